```python
import math
import jax, jax.numpy as jnp
from jax import lax
import numpy as np

D_MODEL = 1024
BATCH = 1
SEQ = 16384
DEPTH = 1

D_MIX = D_MODEL
MLA_HEADS = 8
MLA_NOPE = 64
MLA_ROPE = 32
MLA_V = 64
MLA_Q_RANK = 384
MLA_KV_RANK = 256
ROPE_BASE = 10000.0
MOBA_HEADS = 8
MOBA_HEAD_DIM = 64
MOBA_BLOCK = 256
MOBA_TOPK = 3
Q_BLOCK = 128
D_FF = 2816
FFN_RES = 0.5
EPS = 1e-6
NEG_INF = -1e30

IN_SPLITS = (MLA_Q_RANK, MLA_KV_RANK, MLA_ROPE,
             MOBA_HEADS * MOBA_HEAD_DIM, MOBA_HEADS * MOBA_HEAD_DIM, MOBA_HEADS * MOBA_HEAD_DIM)
D_IN = sum(IN_SPLITS)
D_CAT = MLA_HEADS * MLA_V + MOBA_HEADS * MOBA_HEAD_DIM

kernel_name = "hybrid_mla_moba_macaron"


def rms_norm(x, g):
    xf = x.astype(jnp.float32)
    y = xf * lax.rsqrt(jnp.mean(xf * xf, axis=-1, keepdims=True) + EPS)
    return (y * g.astype(jnp.float32)).astype(x.dtype)


def swiglu(x, w_gate, w_up, w_down):
    return (jax.nn.silu(x @ w_gate) * (x @ w_up)) @ w_down


def apply_rope(x, pos):
    r = x.shape[-1]
    half = r // 2
    inv = jnp.power(ROPE_BASE, -jnp.arange(half, dtype=jnp.float32) * 2.0 / r)
    ang = pos.astype(jnp.float32)[:, :, None, None] * inv
    cos, sin = jnp.cos(ang), jnp.sin(ang)
    xf = x.astype(jnp.float32)
    x1, x2 = xf[..., :half], xf[..., half:]
    out = jnp.concatenate([x1 * cos - x2 * sin, x1 * sin + x2 * cos], axis=-1)
    return out.astype(x.dtype)


def alibi_slopes(n_heads):
    return jnp.exp2(-8.0 * jnp.arange(1, n_heads + 1, dtype=jnp.float32) / n_heads)


def mla_attention(c_q, c_kv, k_rope, pos, q_norm, w_uq, kv_norm, w_ukv):
    B, S, _ = c_q.shape
    H = MLA_HEADS
    q = (rms_norm(c_q, q_norm) @ w_uq).reshape(B, S, H, MLA_NOPE + MLA_ROPE)
    q = jnp.concatenate([q[..., :MLA_NOPE], apply_rope(q[..., MLA_NOPE:], pos)], axis=-1)
    kv = (rms_norm(c_kv, kv_norm) @ w_ukv).reshape(B, S, H, MLA_NOPE + MLA_V)
    k_nope, v = kv[..., :MLA_NOPE], kv[..., MLA_NOPE:]
    k_r = apply_rope(k_rope[:, :, None, :], pos)
    k = jnp.concatenate([k_nope, jnp.broadcast_to(k_r, (B, S, H, MLA_ROPE))], axis=-1)
    scale = (MLA_NOPE + MLA_ROPE) ** -0.5
    key_idx = jnp.arange(S)

    def one_block(start):
        qb = lax.dynamic_slice_in_dim(q, start, Q_BLOCK, axis=1)
        s = jnp.einsum('bqhd,bkhd->bhqk', qb, k).astype(jnp.float32) * scale
        q_idx = start + jnp.arange(Q_BLOCK)
        s = jnp.where(key_idx[None, :] <= q_idx[:, None], s, NEG_INF)
        p = jax.nn.softmax(s, axis=-1).astype(v.dtype)
        return jnp.einsum('bhqk,bkhd->bqhd', p, v)

    out = lax.map(one_block, jnp.arange(0, S, Q_BLOCK))
    return jnp.moveaxis(out, 0, 1).reshape(B, S, H * MLA_V)


def moba_attention(q, k, v, pos):
    B, S, H, D = q.shape
    L = MOBA_BLOCK
    NB = -(-S // L)
    pad = NB * L - S
    kp = jnp.pad(k, ((0, 0), (0, pad), (0, 0), (0, 0)))
    vp = jnp.pad(v, ((0, 0), (0, pad), (0, 0), (0, 0)))
    pp = jnp.pad(pos, ((0, 0), (0, pad)))
    k_sel = min(MOBA_TOPK, NB)
    scale = D ** -0.5
    slopes = alibi_slopes(H)
    kb = kp.reshape(B, NB, L, H, D)
    k_mean = jnp.mean(kb.astype(jnp.float32), axis=2).astype(k.dtype)
    kb_h = kb.transpose(0, 3, 1, 2, 4)
    vb_h = vp.reshape(B, NB, L, H, D).transpose(0, 3, 1, 2, 4)
    pos_b = pp.reshape(B, NB, L)
    bi = jnp.arange(B)[:, None, None, None]
    hi = jnp.arange(H)[None, :, None, None]

    def one_block(start):
        qb = lax.dynamic_slice_in_dim(q, start, Q_BLOCK, axis=1)
        pq = lax.dynamic_slice_in_dim(pos, start, Q_BLOCK, axis=1)
        own = start // L
        gate = jnp.einsum('bqhd,bnhd->bhqn', qb, k_mean).astype(jnp.float32)
        gate = jnp.where((jnp.arange(NB) < own)[None, None, None, :], gate, NEG_INF)
        _, idx = lax.top_k(gate, k_sel)
        valid = jnp.arange(k_sel) < own
        kg = kb_h[bi, hi, idx]
        vg = vb_h[bi, hi, idx]
        pg = pos_b[bi, idx]
        dist_sel = jnp.abs(pq[:, None, :, None, None] - pg).astype(jnp.float32)
        s_sel = jnp.einsum('bqhd,bhqkld->bhqkl', qb, kg).astype(jnp.float32) * scale
        s_sel = s_sel - slopes[None, :, None, None, None] * dist_sel
        s_sel = jnp.where(valid[None, None, None, :, None], s_sel, NEG_INF)
        k_own = lax.dynamic_slice_in_dim(kp, own * L, L, axis=1)
        v_own = lax.dynamic_slice_in_dim(vp, own * L, L, axis=1)
        p_own_pos = lax.dynamic_slice_in_dim(pp, own * L, L, axis=1)
        dist_own = jnp.abs(pq[:, None, :, None] - p_own_pos[:, None, None, :]).astype(jnp.float32)
        s_own = jnp.einsum('bqhd,blhd->bhql', qb, k_own).astype(jnp.float32) * scale
        s_own = s_own - slopes[None, :, None, None] * dist_own
        causal = (own * L + jnp.arange(L))[None, :] <= (start + jnp.arange(Q_BLOCK))[:, None]
        s_own = jnp.where(causal[None, None], s_own, NEG_INF)
        s = jnp.concatenate([s_sel.reshape(B, H, Q_BLOCK, k_sel * L), s_own], axis=-1)
        p = jax.nn.softmax(s, axis=-1).astype(v.dtype)
        p_sel = p[..., :k_sel * L].reshape(B, H, Q_BLOCK, k_sel, L)
        p_o = p[..., k_sel * L:]
        return (jnp.einsum('bhqkl,bhqkld->bqhd', p_sel, vg)
                + jnp.einsum('bhql,blhd->bqhd', p_o, v_own))

    out = lax.map(one_block, jnp.arange(0, S, Q_BLOCK))
    return jnp.moveaxis(out, 0, 1).reshape(B, S, H * D)


def setup_inputs(seed: int = 0) -> dict:
    key = jax.random.key(seed)
    ks = jax.random.split(key, 20)

    def dense(k, fan_in, fan_out):
        return jax.random.normal(k, (DEPTH, fan_in, fan_out), jnp.float32) * fan_in ** -0.5

    def gain(k, n):
        return 1.0 + 0.02 * jax.random.normal(k, (DEPTH, n), jnp.float32)

    x = jax.random.normal(ks[0], (BATCH, SEQ, D_MODEL), jnp.float32)
    positions = jnp.broadcast_to(jnp.arange(SEQ, dtype=jnp.int32)[None, :], (BATCH, SEQ))
    return {
        "x": x,
        "positions": positions,
        "ffn1_norm": gain(ks[1], D_MODEL),
        "ffn1_w_gate": dense(ks[2], D_MODEL, D_FF),
        "ffn1_w_up": dense(ks[3], D_MODEL, D_FF),
        "ffn1_w_down": dense(ks[4], D_FF, D_MODEL),
        "mix_norm": gain(ks[5], D_MODEL),
        "w_in": dense(ks[6], D_MODEL, D_IN),
        "mla_q_norm": gain(ks[7], MLA_Q_RANK),
        "mla_w_uq": dense(ks[8], MLA_Q_RANK, MLA_HEADS * (MLA_NOPE + MLA_ROPE)),
        "mla_kv_norm": gain(ks[9], MLA_KV_RANK),
        "mla_w_ukv": dense(ks[10], MLA_KV_RANK, MLA_HEADS * (MLA_NOPE + MLA_V)),
        "w_out": dense(ks[11], D_CAT, D_MODEL),
        "ffn2_norm": gain(ks[12], D_MODEL),
        "ffn2_w_gate": dense(ks[13], D_MODEL, D_FF),
        "ffn2_w_up": dense(ks[14], D_MODEL, D_FF),
        "ffn2_w_down": dense(ks[15], D_FF, D_MODEL),
        "final_norm": 1.0 + 0.02 * jax.random.normal(ks[16], (D_MODEL,), jnp.float32),
    }


def reference(x, positions, ffn1_norm, ffn1_w_gate, ffn1_w_up, ffn1_w_down, mix_norm, w_in,
              mla_q_norm, mla_w_uq, mla_kv_norm, mla_w_ukv, w_out, ffn2_norm, ffn2_w_gate,
              ffn2_w_up, ffn2_w_down, final_norm):
    B, S, _ = x.shape
    cuts = np.cumsum(IN_SPLITS)[:-1].tolist()
    for l in range(DEPTH):
        x = x + FFN_RES * swiglu(rms_norm(x, ffn1_norm[l]), ffn1_w_gate[l], ffn1_w_up[l], ffn1_w_down[l])
        u = rms_norm(x, mix_norm[l]) @ w_in[l]
        c_q, c_kv, k_rope, mq, mk, mv = jnp.split(u, cuts, axis=-1)
        a_out = mla_attention(c_q, c_kv, k_rope, positions, mla_q_norm[l], mla_w_uq[l],
                              mla_kv_norm[l], mla_w_ukv[l])
        shp = (B, S, MOBA_HEADS, MOBA_HEAD_DIM)
        b_out = moba_attention(mq.reshape(shp), mk.reshape(shp), mv.reshape(shp), positions)
        x = x + jnp.concatenate([a_out, b_out], axis=-1) @ w_out[l]
        x = x + FFN_RES * swiglu(rms_norm(x, ffn2_norm[l]), ffn2_w_gate[l], ffn2_w_up[l], ffn2_w_down[l])
    return rms_norm(x, final_norm)
```

```python
import functools
import math

import jax
import jax.numpy as jnp
from jax import lax
from jax.experimental import pallas as pl
from jax.experimental.pallas import tpu as pltpu

F32 = jnp.float32
BF16 = jnp.bfloat16

D_MODEL = 1024
D_FF = 2816
HEADS = 8
MLA_NOPE = 64
MLA_ROPE = 32
MLA_V = 64
MLA_Q_RANK = 384
MLA_KV_RANK = 256
ROPE_BASE = 10000.0
ROPE_HALF = MLA_ROPE // 2
MOBA_DIM = 64
MOBA_BLOCK = 256
MOBA_TOPK = 3
EPS = 1e-6
NEG_INF = -1e30
LOG2E = math.log2(math.e)

LANES = 128
HEAD_PAD = 128
TILE = MOBA_BLOCK
FFN_ROWS = 512
FFN_CHUNK = 256
VMEM_LIMIT = 56 * 1024 * 1024

C_CQ = 0
C_CKV = C_CQ + MLA_Q_RANK
C_MQ = C_CKV + MLA_KV_RANK
C_MK = C_MQ + HEADS * MOBA_DIM
C_MV = C_MK + HEADS * MOBA_DIM
C_KR = C_MV + HEADS * MOBA_DIM
D_IN_PAD = C_KR + LANES


def _rms(x, g):
    y = x * lax.rsqrt(jnp.mean(x * x, axis=-1, keepdims=True) + EPS)
    return y * g


def _swiglu(h, wg_ref, wu_ref, wd_ref):
    acc = None
    for c in range(D_FF // FFN_CHUNK):
        sl = slice(c * FFN_CHUNK, (c + 1) * FFN_CHUNK)
        g = jnp.dot(h, wg_ref[:, sl], preferred_element_type=F32)
        u = jnp.dot(h, wu_ref[:, sl], preferred_element_type=F32)
        a = (g * jax.nn.sigmoid(g) * u).astype(BF16)
        d = jnp.dot(a, wd_ref[sl, :], preferred_element_type=F32)
        acc = d if acc is None else acc + d
    return acc


def _ffn1_kernel(x_ref, g_ref, wg_ref, wu_ref, wd_ref, o_ref):
    x = x_ref[...]
    h = _rms(x, g_ref[...]).astype(BF16)
    o_ref[...] = x + 0.5 * _swiglu(h, wg_ref, wu_ref, wd_ref)


def _mix_ffn2_kernel(x_ref, at_ref, bt_ref, wo_ref, g_ref, wg_ref, wu_ref, wd_ref, gf_ref, o_ref):
    half = HEADS * MLA_V
    tn = (((0,), (0,)), ((), ()))
    parts = []
    for j in range(FFN_ROWS // TILE):
        pa = lax.dot_general(at_ref[j], wo_ref[0:half, :], tn, preferred_element_type=F32)
        pb = lax.dot_general(bt_ref[j], wo_ref[half:2 * half, :], tn, preferred_element_type=F32)
        parts.append(pa + pb)
    x = x_ref[...] + jnp.concatenate(parts, axis=0)
    h = _rms(x, g_ref[...]).astype(BF16)
    y = x + 0.5 * _swiglu(h, wg_ref, wu_ref, wd_ref)
    o_ref[...] = _rms(y, gf_ref[...])


def _proj_kernel(x_ref, pos_ref, inv_ref, gm_ref, win_ref, gq_ref, wuq_ref, gkv_ref, wk_ref, wv_ref,
                 qa_ref, ka_ref, va_ref, qb_ref, kb_ref, vb_ref, sel_ref, km_ref):
    i = pl.program_id(0)
    nb = km_ref.shape[0]

    @pl.when(i == 0)
    def _():
        km_ref[...] = jnp.zeros_like(km_ref)

    h = _rms(x_ref[...], gm_ref[...]).astype(BF16)
    u = jnp.dot(h, win_ref[...], preferred_element_type=F32)

    ang = inv_ref[...] * pos_ref[...].astype(F32)
    cos, sin = jnp.cos(ang), jnp.sin(ang)

    def rope_t(xt):
        x1, x2 = xt[:ROPE_HALF], xt[ROPE_HALF:]
        return x1 * cos - x2 * sin, x1 * sin + x2 * cos

    cqn = _rms(u[:, C_CQ:C_CQ + MLA_Q_RANK], gq_ref[...]).astype(BF16)
    q = jnp.dot(cqn, wuq_ref[...], preferred_element_type=F32)
    qt = q.T * ((MLA_NOPE + MLA_ROPE) ** -0.5 * LOG2E)
    n_nope = HEADS * MLA_NOPE
    zpad = jnp.zeros((HEAD_PAD - MLA_NOPE - MLA_ROPE, TILE), BF16)
    for hd in range(HEADS):
        r0 = hd * HEAD_PAD
        qa_ref[0, r0:r0 + MLA_NOPE, :] = qt[hd * MLA_NOPE:(hd + 1) * MLA_NOPE].astype(BF16)
        r1, r2 = rope_t(qt[n_nope + hd * MLA_ROPE:n_nope + (hd + 1) * MLA_ROPE])
        qa_ref[0, r0 + MLA_NOPE:r0 + MLA_NOPE + ROPE_HALF, :] = r1.astype(BF16)
        qa_ref[0, r0 + MLA_NOPE + ROPE_HALF:r0 + MLA_NOPE + MLA_ROPE, :] = r2.astype(BF16)
        qa_ref[0, r0 + MLA_NOPE + MLA_ROPE:r0 + HEAD_PAD, :] = zpad

    ckvn = _rms(u[:, C_CKV:C_CKV + MLA_KV_RANK], gkv_ref[...]).astype(BF16)
    kn = jnp.dot(ckvn, wk_ref[...], preferred_element_type=F32)
    v = jnp.dot(ckvn, wv_ref[...], preferred_element_type=F32)
    krt = u[:, C_KR:C_KR + LANES].T
    k1, k2 = rope_t(krt[:MLA_ROPE])
    placed = jnp.concatenate(
        [jnp.zeros((MLA_NOPE, TILE), F32), k1, k2,
         jnp.zeros((HEAD_PAD - MLA_NOPE - MLA_ROPE, TILE), F32)], axis=0).T
    for hd in range(HEADS):
        c0 = hd * HEAD_PAD
        ka_ref[:, c0:c0 + HEAD_PAD] = (kn[:, c0:c0 + HEAD_PAD] + placed).astype(BF16)
    va_ref[0] = v.T.astype(BF16)

    mq = u[:, C_MQ:C_MQ + HEADS * MOBA_DIM]
    mk = u[:, C_MK:C_MK + HEADS * MOBA_DIM]
    mv = u[:, C_MV:C_MV + HEADS * MOBA_DIM]
    mqt = mq.T
    mqs = (mqt * (MOBA_DIM ** -0.5 * LOG2E)).astype(BF16)
    zhalf = jnp.zeros((MOBA_DIM, TILE), BF16)
    for hd in range(HEADS):
        r0 = hd * HEAD_PAD
        own = mqs[hd * MOBA_DIM:(hd + 1) * MOBA_DIM]
        lo, hi = (own, zhalf) if hd % 2 == 0 else (zhalf, own)
        qb_ref[0, r0:r0 + MOBA_DIM, :] = lo
        qb_ref[0, r0 + MOBA_DIM:r0 + HEAD_PAD, :] = hi
    kb_ref[...] = mk.astype(BF16)
    vb_ref[0] = mv.T.astype(BF16)

    row = lax.broadcasted_iota(jnp.int32, (nb, TILE), 0)
    for hd in range(HEADS):
        gate = jnp.dot(km_ref[:, hd * MOBA_DIM:(hd + 1) * MOBA_DIM],
                       mqt[hd * MOBA_DIM:(hd + 1) * MOBA_DIM],
                       preferred_element_type=F32, precision=lax.Precision.HIGHEST)
        g = jnp.where(row < i, gate, NEG_INF)
        chosen = row == i
        for r in range(MOBA_TOPK):
            mx = jnp.max(g, axis=0, keepdims=True)
            idx = jnp.min(jnp.where(g == mx, row, nb), axis=0, keepdims=True)
            hit = row == idx
            chosen = jnp.logical_or(chosen, jnp.logical_and(hit, i > r))
            g = jnp.where(hit, -jnp.inf, g)
        sel_ref[hd] = jnp.where(chosen, 0.0, NEG_INF).astype(F32)
    km_ref[pl.ds(i, 1), :] = jnp.mean(mk, axis=0, keepdims=True)


def _online_step(s, vt, m, l, acc):
    m_new = jnp.maximum(m, jnp.max(s, axis=0, keepdims=True))
    p = jnp.exp2(s - m_new)
    alpha = jnp.exp2(m - m_new)
    l = alpha * l + jnp.sum(p, axis=0, keepdims=True)
    acc = alpha * acc + jnp.dot(vt, p.astype(BF16), preferred_element_type=F32)
    return m_new, l, acc


def _causal_mask():
    key = lax.broadcasted_iota(jnp.int32, (TILE, TILE), 0)
    qry = lax.broadcasted_iota(jnp.int32, (TILE, TILE), 1)
    return key <= qry


def _mla_kernel(qt_ref, k_ref, vt_ref, o_ref):
    qi = pl.program_id(1)
    qt = qt_ref[0]
    init = (jnp.full((1, TILE), NEG_INF, F32), jnp.zeros((1, TILE), F32),
            jnp.zeros((MLA_V, TILE), F32))

    def past(n, carry):
        s = jnp.dot(k_ref[n], qt, preferred_element_type=F32)
        return _online_step(s, vt_ref[n], *carry)

    carry = lax.fori_loop(0, qi, past, init)
    s = jnp.dot(k_ref[qi], qt, preferred_element_type=F32)
    s = jnp.where(_causal_mask(), s, NEG_INF)
    m, l, acc = _online_step(s, vt_ref[qi], *carry)
    o_ref[0] = (acc / l).astype(o_ref.dtype)


def _moba_kernel(slope_ref, qt_ref, k_ref, vt_ref, sel_ref, pq_ref, pk_ref, o_ref):
    hd = pl.program_id(0)
    qi = pl.program_id(1)
    qt = qt_ref[0]
    slope = slope_ref[hd] * LOG2E
    pq = pq_ref[...]

    def scores(n):
        s = jnp.dot(k_ref[n], qt, preferred_element_type=F32)
        pk = pk_ref[n]
        pk = jnp.concatenate([pk] * (TILE // LANES), axis=1)
        return s - slope * jnp.abs(pk - pq)

    init = (jnp.full((1, TILE), NEG_INF, F32), jnp.zeros((1, TILE), F32),
            jnp.zeros((MOBA_DIM, TILE), F32))
    s = jnp.where(_causal_mask(), scores(qi), NEG_INF)
    carry = _online_step(s, vt_ref[qi], *init)

    def past(n, carry):
        s = scores(n) + sel_ref[0, pl.ds(n, 1), :]
        return _online_step(s, vt_ref[n], *carry)

    m, l, acc = lax.fori_loop(0, qi, past, carry)
    o_ref[0] = (acc / l).astype(o_ref.dtype)


def _const_spec(shape):
    zeros = (0,) * len(shape)
    return pl.BlockSpec(shape, lambda *_: zeros)


def _params(n_axes):
    return pltpu.CompilerParams(dimension_semantics=("arbitrary",) * n_axes,
                                vmem_limit_bytes=VMEM_LIMIT)


def _ffn1(x, g, wg, wu, wd):
    s = x.shape[0]
    row = pl.BlockSpec((FFN_ROWS, D_MODEL), lambda i: (i, 0))
    return pl.pallas_call(
        _ffn1_kernel,
        out_shape=jax.ShapeDtypeStruct((s, D_MODEL), F32),
        grid=(s // FFN_ROWS,),
        in_specs=[row, _const_spec((1, D_MODEL)), _const_spec((D_MODEL, D_FF)),
                  _const_spec((D_MODEL, D_FF)), _const_spec((D_FF, D_MODEL))],
        out_specs=row,
        compiler_params=_params(1),
        name="ffn1",
    )(x, g, wg, wu, wd)


def _mix_ffn2(x, at, bt, wo, g, wg, wu, wd, gf):
    s = x.shape[0]
    row = pl.BlockSpec((FFN_ROWS, D_MODEL), lambda i: (i, 0))
    headt = pl.BlockSpec((FFN_ROWS // TILE, HEADS * MLA_V, TILE), lambda i: (i, 0, 0))
    return pl.pallas_call(
        _mix_ffn2_kernel,
        out_shape=jax.ShapeDtypeStruct((s, D_MODEL), F32),
        grid=(s // FFN_ROWS,),
        in_specs=[row, headt, headt, _const_spec((D_MODEL, D_MODEL)), _const_spec((1, D_MODEL)),
                  _const_spec((D_MODEL, D_FF)), _const_spec((D_MODEL, D_FF)),
                  _const_spec((D_FF, D_MODEL)), _const_spec((1, D_MODEL))],
        out_specs=row,
        compiler_params=_params(1),
        name="mix_ffn2",
    )(x, at, bt, wo, g, wg, wu, wd, gf)


def _proj(x, pos_row, inv, gm, win, gq, wuq, gkv, wk, wv):
    s = x.shape[0]
    nb = s // TILE
    tok_t = lambda rows: pl.BlockSpec((1, rows, TILE), lambda i: (i, 0, 0))
    tok = lambda cols: pl.BlockSpec((TILE, cols), lambda i: (i, 0))
    out_shape = (
        jax.ShapeDtypeStruct((nb, HEADS * HEAD_PAD, TILE), BF16),
        jax.ShapeDtypeStruct((s, HEADS * HEAD_PAD), BF16),
        jax.ShapeDtypeStruct((nb, HEADS * MLA_V, TILE), BF16),
        jax.ShapeDtypeStruct((nb, HEADS * HEAD_PAD, TILE), BF16),
        jax.ShapeDtypeStruct((s, HEADS * MOBA_DIM), BF16),
        jax.ShapeDtypeStruct((nb, HEADS * MOBA_DIM, TILE), BF16),
        jax.ShapeDtypeStruct((HEADS, nb, s), F32),
    )
    out_specs = (
        tok_t(HEADS * HEAD_PAD), tok(HEADS * HEAD_PAD), tok_t(HEADS * MLA_V),
        tok_t(HEADS * HEAD_PAD), tok(HEADS * MOBA_DIM), tok_t(HEADS * MOBA_DIM),
        pl.BlockSpec((HEADS, nb, TILE), lambda i: (0, 0, i)),
    )
    in_specs = [
        tok(D_MODEL), pl.BlockSpec((1, TILE), lambda i: (0, i)), _const_spec((ROPE_HALF, 1)),
        _const_spec((1, D_MODEL)), _const_spec((D_MODEL, D_IN_PAD)),
        _const_spec((1, MLA_Q_RANK)), _const_spec((MLA_Q_RANK, HEADS * (MLA_NOPE + MLA_ROPE))),
        _const_spec((1, MLA_KV_RANK)), _const_spec((MLA_KV_RANK, HEADS * HEAD_PAD)),
        _const_spec((MLA_KV_RANK, HEADS * MLA_V)),
    ]
    return pl.pallas_call(
        _proj_kernel,
        out_shape=out_shape,
        grid=(nb,),
        in_specs=in_specs,
        out_specs=out_specs,
        scratch_shapes=[pltpu.VMEM((nb, HEADS * MOBA_DIM), F32)],
        compiler_params=_params(1),
        name="proj",
    )(x, pos_row, inv, gm, win, gq, wuq, gkv, wk, wv)


def _mla_attn(qt, k, vt):
    nb = qt.shape[0]
    k3 = k.reshape(nb, TILE, HEADS * HEAD_PAD)
    return pl.pallas_call(
        _mla_kernel,
        out_shape=jax.ShapeDtypeStruct((nb, HEADS * MLA_V, TILE), BF16),
        grid=(HEADS, nb),
        in_specs=[pl.BlockSpec((1, HEAD_PAD, TILE), lambda h, i: (i, h, 0)),
                  pl.BlockSpec((nb, TILE, HEAD_PAD), lambda h, i: (0, 0, h)),
                  pl.BlockSpec((nb, MLA_V, TILE), lambda h, i: (0, h, 0))],
        out_specs=pl.BlockSpec((1, MLA_V, TILE), lambda h, i: (i, h, 0)),
        compiler_params=_params(2),
        name="mla_attn",
    )(qt, k3, vt)


def _moba_attn(slopes, qt, k, vt, sel, pos_row, pos_rep):
    nb = qt.shape[0]
    k3 = k.reshape(nb, TILE, HEADS * MOBA_DIM)
    pk3 = pos_rep.reshape(nb, TILE, LANES)
    return pl.pallas_call(
        _moba_kernel,
        out_shape=jax.ShapeDtypeStruct((nb, HEADS * MOBA_DIM, TILE), BF16),
        grid=(HEADS, nb),
        in_specs=[pl.BlockSpec(memory_space=pltpu.SMEM),
                  pl.BlockSpec((1, HEAD_PAD, TILE), lambda h, i: (i, h, 0)),
                  pl.BlockSpec((nb, TILE, 2 * MOBA_DIM), lambda h, i: (0, 0, h // 2)),
                  pl.BlockSpec((nb, MOBA_DIM, TILE), lambda h, i: (0, h, 0)),
                  pl.BlockSpec((1, nb, TILE), lambda h, i: (h, 0, i)),
                  pl.BlockSpec((1, TILE), lambda h, i: (0, i)),
                  pl.BlockSpec((nb, TILE, LANES), lambda h, i: (0, 0, 0))],
        out_specs=pl.BlockSpec((1, MOBA_DIM, TILE), lambda h, i: (i, h, 0)),
        compiler_params=_params(2),
        name="moba_attn",
    )(slopes, qt, k3, vt, sel, pos_row, pk3)


def kernel(x, positions, ffn1_norm, ffn1_w_gate, ffn1_w_up, ffn1_w_down, mix_norm, w_in, mla_q_norm,
           mla_w_uq, mla_kv_norm, mla_w_ukv, w_out, ffn2_norm, ffn2_w_gate, ffn2_w_up, ffn2_w_down,
           final_norm):
    b, s, d = x.shape
    assert b == 1 and d == D_MODEL and s % FFN_ROWS == 0 and w_in.shape[0] == 1
    xs = x[0]
    pos_row = positions.reshape(1, s)
    pos_f = positions.reshape(s, 1).astype(F32)
    pos_rep = jnp.broadcast_to(pos_f, (s, LANES))
    inv = jnp.power(ROPE_BASE, -jnp.arange(ROPE_HALF, dtype=F32) * 2.0 / MLA_ROPE).reshape(ROPE_HALF, 1)
    slopes = jnp.exp2(-8.0 * jnp.arange(1, HEADS + 1, dtype=F32) / HEADS)

    wi = w_in[0]
    c_kr = MLA_Q_RANK + MLA_KV_RANK
    win = jnp.concatenate(
        [wi[:, :c_kr], wi[:, c_kr + MLA_ROPE:], wi[:, c_kr:c_kr + MLA_ROPE],
         jnp.zeros((D_MODEL, LANES - MLA_ROPE), F32)], axis=1).astype(BF16)
    wq3 = mla_w_uq[0].reshape(MLA_Q_RANK, HEADS, MLA_NOPE + MLA_ROPE)
    wuq = jnp.concatenate([wq3[:, :, :MLA_NOPE].reshape(MLA_Q_RANK, -1),
                           wq3[:, :, MLA_NOPE:].reshape(MLA_Q_RANK, -1)], axis=1).astype(BF16)
    wkv3 = mla_w_ukv[0].reshape(MLA_KV_RANK, HEADS, MLA_NOPE + MLA_V)
    wk = jnp.pad(wkv3[:, :, :MLA_NOPE], ((0, 0), (0, 0), (0, HEAD_PAD - MLA_NOPE))
                 ).reshape(MLA_KV_RANK, -1).astype(BF16)
    wv = wkv3[:, :, MLA_NOPE:].reshape(MLA_KV_RANK, -1).astype(BF16)
    row = lambda g: g.reshape(1, -1)

    x1 = _ffn1(xs, row(ffn1_norm[0]), ffn1_w_gate[0].astype(BF16), ffn1_w_up[0].astype(BF16),
               ffn1_w_down[0].astype(BF16))
    qa, ka, va, qb, kb, vb, sel = _proj(x1, pos_row, inv, row(mix_norm[0]), win,
                                        row(mla_q_norm[0]), wuq, row(mla_kv_norm[0]), wk, wv)
    at = _mla_attn(qa, ka, va)
    bt = _moba_attn(slopes, qb, kb, vb, sel, pos_row.astype(F32), pos_rep)
    out = _mix_ffn2(x1, at, bt, w_out[0].astype(BF16), row(ffn2_norm[0]), ffn2_w_gate[0].astype(BF16),
                    ffn2_w_up[0].astype(BF16), ffn2_w_down[0].astype(BF16), row(final_norm))
    return out[None]
```

```python
import math

import jax
import jax.numpy as jnp
from jax import lax
from jax.experimental import pallas as pl
from jax.experimental.pallas import tpu as pltpu

F32 = jnp.float32
BF16 = jnp.bfloat16

D_MODEL = 1024
D_FF = 2816
HEADS = 8
MLA_NOPE = 64
MLA_ROPE = 32
MLA_V = 64
MLA_Q_RANK = 384
MLA_KV_RANK = 256
ROPE_BASE = 10000.0
ROPE_HALF = MLA_ROPE // 2
MOBA_DIM = 64
MOBA_BLOCK = 256
MOBA_TOPK = 3
EPS = 1e-6
NEG_INF = -1e30
LOG2E = math.log2(math.e)

LANES = 128
TILE = MOBA_BLOCK
FFN_ROWS = 512
FFN_CHUNK = 256
VMEM_LIMIT = 56 * 1024 * 1024
LOOKAHEAD = 4
CARRY = 4
HEAD_W = 64
ONES_ROWS = 16
ACC_ROWS = HEAD_W + ONES_ROWS
assert MLA_NOPE == HEAD_W and MOBA_DIM == HEAD_W and MLA_V == HEAD_W and LOOKAHEAD < HEADS

C_CQ = 0
C_CKV = C_CQ + MLA_Q_RANK
C_MQ = C_CKV + MLA_KV_RANK
C_MK = C_MQ + HEADS * MOBA_DIM
C_MV = C_MK + HEADS * MOBA_DIM
C_KR = C_MV + HEADS * MOBA_DIM
D_IN_PAD = C_KR + LANES


def _rms(x, g):
    y = x * lax.rsqrt(jnp.mean(x * x, axis=-1, keepdims=True) + EPS)
    return y * g


def _swiglu(h, wg_ref, wu_ref, wd_ref):
    acc = None
    for c in range(D_FF // FFN_CHUNK):
        sl = slice(c * FFN_CHUNK, (c + 1) * FFN_CHUNK)
        g = jnp.dot(h, wg_ref[:, sl], preferred_element_type=F32)
        u = jnp.dot(h, wu_ref[:, sl], preferred_element_type=F32)
        a = (g * jax.nn.sigmoid(g) * u).astype(BF16)
        d = jnp.dot(a, wd_ref[sl, :], preferred_element_type=F32)
        acc = d if acc is None else acc + d
    return acc


def _ffn1_kernel(x_ref, g_ref, wg_ref, wu_ref, wd_ref, o_ref):
    x = x_ref[...]
    h = _rms(x, g_ref[...]).astype(BF16)
    o_ref[...] = x + 0.5 * _swiglu(h, wg_ref, wu_ref, wd_ref)


def _mix_ffn2_kernel(x_ref, at_ref, bt_ref, wo_ref, g_ref, wg_ref, wu_ref, wd_ref, gf_ref, o_ref):
    half = HEADS * MLA_V
    tn = (((0,), (0,)), ((), ()))
    parts = []
    for j in range(FFN_ROWS // TILE):
        pa = lax.dot_general(at_ref[j], wo_ref[0:half, :], tn, preferred_element_type=F32)
        pb = lax.dot_general(bt_ref[j], wo_ref[half:2 * half, :], tn, preferred_element_type=F32)
        parts.append(pa + pb)
    x = x_ref[...] + jnp.concatenate(parts, axis=0)
    h = _rms(x, g_ref[...]).astype(BF16)
    y = x + 0.5 * _swiglu(h, wg_ref, wu_ref, wd_ref)
    o_ref[...] = _rms(y, gf_ref[...])


def _proj_kernel(x_ref, pos_ref, inv_ref, gm_ref, win_ref, gq_ref, wuq_ref, gkv_ref, wk_ref, wv_ref,
                 qa_ref, ka_ref, kr_ref, va_ref, qb_ref, kb_ref, vb_ref, sel_ref, km_ref):
    i = pl.program_id(0)
    nb = km_ref.shape[0]

    @pl.when(i == 0)
    def _():
        km_ref[...] = jnp.zeros_like(km_ref)

    h = _rms(x_ref[...], gm_ref[...]).astype(BF16)
    u = jnp.dot(h, win_ref[...], preferred_element_type=F32)

    ang = inv_ref[...] * pos_ref[...].astype(F32)
    cos, sin = jnp.cos(ang), jnp.sin(ang)

    def rope_t(xt):
        x1, x2 = xt[:ROPE_HALF], xt[ROPE_HALF:]
        return x1 * cos - x2 * sin, x1 * sin + x2 * cos

    cqn = _rms(u[:, C_CQ:C_CQ + MLA_Q_RANK], gq_ref[...]).astype(BF16)
    q = jnp.dot(cqn, wuq_ref[...], preferred_element_type=F32)
    qt = q.T * ((MLA_NOPE + MLA_ROPE) ** -0.5 * LOG2E)
    n_nope = HEADS * MLA_NOPE
    qa_ref[0, 0:n_nope, :] = qt[:n_nope].astype(BF16)
    for hd in range(HEADS):
        r0 = n_nope + hd * MLA_ROPE
        r1, r2 = rope_t(qt[r0:r0 + MLA_ROPE])
        qa_ref[0, r0:r0 + ROPE_HALF, :] = r1.astype(BF16)
        qa_ref[0, r0 + ROPE_HALF:r0 + MLA_ROPE, :] = r2.astype(BF16)

    ckvn = _rms(u[:, C_CKV:C_CKV + MLA_KV_RANK], gkv_ref[...]).astype(BF16)
    ka_ref[...] = jnp.dot(ckvn, wk_ref[...], preferred_element_type=F32).astype(BF16)
    v = jnp.dot(ckvn, wv_ref[...], preferred_element_type=F32)
    krt = u[:, C_KR:C_KR + LANES].T
    k1, k2 = rope_t(krt[:MLA_ROPE])
    kr_ref[...] = jnp.concatenate(
        [k1, k2, jnp.zeros((LANES - MLA_ROPE, TILE), F32)], axis=0).T.astype(BF16)
    va_ref[0] = v.T.astype(BF16)

    mq = u[:, C_MQ:C_MQ + HEADS * MOBA_DIM]
    mk = u[:, C_MK:C_MK + HEADS * MOBA_DIM]
    mv = u[:, C_MV:C_MV + HEADS * MOBA_DIM]
    mqt = mq.T
    qb_ref[0] = (mqt * (MOBA_DIM ** -0.5 * LOG2E)).astype(BF16)
    kb_ref[...] = mk.astype(BF16)
    vb_ref[0] = mv.T.astype(BF16)

    row = lax.broadcasted_iota(jnp.int32, (nb, TILE), 0)
    for hd in range(HEADS):
        gate = jnp.dot(km_ref[:, hd * MOBA_DIM:(hd + 1) * MOBA_DIM],
                       mqt[hd * MOBA_DIM:(hd + 1) * MOBA_DIM],
                       preferred_element_type=F32, precision=lax.Precision.HIGHEST)
        g = jnp.where(row < i, gate, NEG_INF)
        chosen = row == i
        for r in range(MOBA_TOPK):
            mx = jnp.max(g, axis=0, keepdims=True)
            idx = jnp.min(jnp.where(g == mx, row, nb), axis=0, keepdims=True)
            hit = row == idx
            chosen = jnp.logical_or(chosen, jnp.logical_and(hit, i > r))
            g = jnp.where(hit, -jnp.inf, g)
        sel_ref[hd] = jnp.where(chosen, 0.0, NEG_INF).astype(F32)
    km_ref[pl.ds(i, 1), :] = jnp.mean(mk, axis=0, keepdims=True)


def _online_update(hd, s, vt, m_ref, acc_ref):
    m_old = m_ref[hd]
    m_new = jnp.maximum(m_old, jnp.max(s, axis=0, keepdims=True))
    p = jnp.exp2(s - m_new).astype(BF16)
    alpha = jnp.exp2(m_old - m_new)
    vt1 = jnp.concatenate([vt, jnp.ones((ONES_ROWS, TILE), BF16)], axis=0)
    acc_ref[hd] = alpha * acc_ref[hd] + jnp.dot(vt1, p, preferred_element_type=F32)
    m_ref[hd] = m_new


def _init_state(m_ref, acc_ref):
    m_ref[...] = jnp.full(m_ref.shape, NEG_INF, F32)
    acc_ref[...] = jnp.zeros(acc_ref.shape, F32)


def _write_out(o_ref, acc_ref):
    for hd in range(HEADS):
        acc = acc_ref[hd]
        o_ref[0, hd * HEAD_W:(hd + 1) * HEAD_W, :] = (acc[:HEAD_W] / acc[HEAD_W:HEAD_W + 1]).astype(o_ref.dtype)


def _pair_pad(x, hd):
    z = jnp.zeros_like(x)
    return jnp.concatenate([x, z] if hd % 2 == 0 else [z, x], axis=0)


def _pipelined_tile(scores_now, scores_next, finish, s_ref):
    fresh = {hd: scores_now(hd) for hd in range(CARRY, LOOKAHEAD)}
    for t in range(HEADS):
        ahead = t + LOOKAHEAD
        if ahead < HEADS:
            fresh[ahead] = scores_now(ahead)
        elif ahead - HEADS < CARRY and scores_next is not None:
            s_ref[ahead - HEADS] = scores_next(ahead - HEADS)
        finish(t, s_ref[t] if t < CARRY else fresh.pop(t))


def _causal_mask():
    key = lax.broadcasted_iota(jnp.int32, (TILE, TILE), 0)
    qry = lax.broadcasted_iota(jnp.int32, (TILE, TILE), 1)
    return key <= qry


def _mla_kernel(qt_ref, kn_ref, kr_ref, vt_ref, o_ref, qp_ref, m_ref, acc_ref, s_ref):
    qi = pl.program_id(0)
    n_nope = HEADS * MLA_NOPE
    zrope = jnp.zeros((LANES - MLA_ROPE, TILE), BF16)
    for hd in range(HEADS):
        nope = qt_ref[0, hd * MLA_NOPE:(hd + 1) * MLA_NOPE, :]
        rope = qt_ref[0, n_nope + hd * MLA_ROPE:n_nope + (hd + 1) * MLA_ROPE, :]
        qp_ref[hd] = jnp.concatenate([_pair_pad(nope, hd), rope, zrope], axis=0)
    _init_state(m_ref, acc_ref)

    def scores(n):
        def f(hd):
            j = hd // 2
            kcat = jnp.concatenate([kn_ref[n, :, j * LANES:(j + 1) * LANES], kr_ref[n]], axis=1)
            return jnp.dot(kcat, qp_ref[hd], preferred_element_type=F32)
        return f

    def finish(n, mask):
        def f(hd, s):
            if mask is not None:
                s = jnp.where(mask, s, NEG_INF)
            _online_update(hd, s, vt_ref[n, hd * HEAD_W:(hd + 1) * HEAD_W, :], m_ref, acc_ref)
        return f

    for hd in range(CARRY):
        s_ref[hd] = scores(0)(hd)

    def past(n, c):
        _pipelined_tile(scores(n), scores(n + 1), finish(n, None), s_ref)
        return c

    lax.fori_loop(0, qi, past, 0)
    _pipelined_tile(scores(qi), None, finish(qi, _causal_mask()), s_ref)
    _write_out(o_ref, acc_ref)


def _moba_kernel(slope_ref, qt_ref, k_ref, vt_ref, sel_ref, pq_ref, pk_ref, o_ref,
                 qp_ref, m_ref, acc_ref, s_ref):
    qi = pl.program_id(0)
    for hd in range(HEADS):
        qp_ref[hd] = _pair_pad(qt_ref[0, hd * MOBA_DIM:(hd + 1) * MOBA_DIM, :], hd)
    _init_state(m_ref, acc_ref)
    pq = pq_ref[...]

    def scores(n):
        def f(hd):
            j = hd // 2
            return jnp.dot(k_ref[n, :, j * LANES:(j + 1) * LANES], qp_ref[hd], preferred_element_type=F32)
        return f

    def finish(n, own):
        pk = pk_ref[n]
        dist = jnp.abs(jnp.concatenate([pk] * (TILE // LANES), axis=1) - pq)
        mask = _causal_mask() if own else None

        def f(hd, s):
            s = s - (slope_ref[hd] * LOG2E) * dist
            if own:
                s = jnp.where(mask, s, NEG_INF)
            else:
                s = s + sel_ref[hd, pl.ds(n, 1), :]
            _online_update(hd, s, vt_ref[n, hd * HEAD_W:(hd + 1) * HEAD_W, :], m_ref, acc_ref)
        return f

    for hd in range(CARRY):
        s_ref[hd] = scores(qi)(hd)
    _pipelined_tile(scores(qi), scores(0), finish(qi, True), s_ref)

    def past(n, c):
        _pipelined_tile(scores(n), scores(n + 1), finish(n, False), s_ref)
        return c

    lax.fori_loop(0, qi, past, 0)
    _write_out(o_ref, acc_ref)


def _const_spec(shape, single=False):
    zeros = (0,) * len(shape)
    if single:
        return pl.BlockSpec(shape, lambda *_: zeros, pipeline_mode=pl.Buffered(1))
    return pl.BlockSpec(shape, lambda *_: zeros)


def _params(n_axes):
    return pltpu.CompilerParams(dimension_semantics=("arbitrary",) * n_axes,
                                vmem_limit_bytes=VMEM_LIMIT)


def _ffn1(x, g, wg, wu, wd):
    s = x.shape[0]
    row = pl.BlockSpec((FFN_ROWS, D_MODEL), lambda i: (i, 0))
    return pl.pallas_call(
        _ffn1_kernel,
        out_shape=jax.ShapeDtypeStruct((s, D_MODEL), F32),
        grid=(s // FFN_ROWS,),
        in_specs=[row, _const_spec((1, D_MODEL)), _const_spec((D_MODEL, D_FF)),
                  _const_spec((D_MODEL, D_FF)), _const_spec((D_FF, D_MODEL))],
        out_specs=row,
        compiler_params=_params(1),
        name="ffn1",
    )(x, g, wg, wu, wd)


def _mix_ffn2(x, at, bt, wo, g, wg, wu, wd, gf):
    s = x.shape[0]
    row = pl.BlockSpec((FFN_ROWS, D_MODEL), lambda i: (i, 0))
    headt = pl.BlockSpec((FFN_ROWS // TILE, HEADS * MLA_V, TILE), lambda i: (i, 0, 0))
    return pl.pallas_call(
        _mix_ffn2_kernel,
        out_shape=jax.ShapeDtypeStruct((s, D_MODEL), F32),
        grid=(s // FFN_ROWS,),
        in_specs=[row, headt, headt, _const_spec((D_MODEL, D_MODEL)), _const_spec((1, D_MODEL)),
                  _const_spec((D_MODEL, D_FF)), _const_spec((D_MODEL, D_FF)),
                  _const_spec((D_FF, D_MODEL)), _const_spec((1, D_MODEL))],
        out_specs=row,
        compiler_params=_params(1),
        name="mix_ffn2",
    )(x, at, bt, wo, g, wg, wu, wd, gf)


def _proj(x, pos_row, inv, gm, win, gq, wuq, gkv, wk, wv):
    s = x.shape[0]
    nb = s // TILE
    tok_t = lambda rows: pl.BlockSpec((1, rows, TILE), lambda i: (i, 0, 0))
    tok = lambda cols: pl.BlockSpec((TILE, cols), lambda i: (i, 0))
    n_q = HEADS * (MLA_NOPE + MLA_ROPE)
    out_shape = (
        jax.ShapeDtypeStruct((nb, n_q, TILE), BF16),
        jax.ShapeDtypeStruct((s, HEADS * MLA_NOPE), BF16),
        jax.ShapeDtypeStruct((s, LANES), BF16),
        jax.ShapeDtypeStruct((nb, HEADS * MLA_V, TILE), BF16),
        jax.ShapeDtypeStruct((nb, HEADS * MOBA_DIM, TILE), BF16),
        jax.ShapeDtypeStruct((s, HEADS * MOBA_DIM), BF16),
        jax.ShapeDtypeStruct((nb, HEADS * MOBA_DIM, TILE), BF16),
        jax.ShapeDtypeStruct((nb, HEADS, nb, TILE), F32),
    )
    out_specs = (
        tok_t(n_q), tok(HEADS * MLA_NOPE), tok(LANES), tok_t(HEADS * MLA_V),
        tok_t(HEADS * MOBA_DIM), tok(HEADS * MOBA_DIM), tok_t(HEADS * MOBA_DIM),
        pl.BlockSpec((None, HEADS, nb, TILE), lambda i: (i, 0, 0, 0)),
    )
    in_specs = [
        tok(D_MODEL), pl.BlockSpec((1, TILE), lambda i: (0, i)), _const_spec((ROPE_HALF, 1)),
        _const_spec((1, D_MODEL)), _const_spec((D_MODEL, D_IN_PAD)),
        _const_spec((1, MLA_Q_RANK)), _const_spec((MLA_Q_RANK, n_q)),
        _const_spec((1, MLA_KV_RANK)), _const_spec((MLA_KV_RANK, HEADS * MLA_NOPE)),
        _const_spec((MLA_KV_RANK, HEADS * MLA_V)),
    ]
    return pl.pallas_call(
        _proj_kernel,
        out_shape=out_shape,
        grid=(nb,),
        in_specs=in_specs,
        out_specs=out_specs,
        scratch_shapes=[pltpu.VMEM((nb, HEADS * MOBA_DIM), F32)],
        compiler_params=_params(1),
        name="proj",
    )(x, pos_row, inv, gm, win, gq, wuq, gkv, wk, wv)


def _attn_scratch(q_rows):
    return [pltpu.VMEM((HEADS, q_rows, TILE), BF16), pltpu.VMEM((HEADS, 1, TILE), F32),
            pltpu.VMEM((HEADS, ACC_ROWS, TILE), F32), pltpu.VMEM((CARRY, TILE, TILE), F32)]


def _mla_attn(qt, kn, kr, vt):
    nb = qt.shape[0]
    n_q = HEADS * (MLA_NOPE + MLA_ROPE)
    return pl.pallas_call(
        _mla_kernel,
        out_shape=jax.ShapeDtypeStruct((nb, HEADS * MLA_V, TILE), BF16),
        grid=(nb,),
        in_specs=[pl.BlockSpec((1, n_q, TILE), lambda i: (i, 0, 0)),
                  _const_spec((nb, TILE, HEADS * MLA_NOPE), single=True),
                  _const_spec((nb, TILE, LANES), single=True),
                  _const_spec((nb, HEADS * MLA_V, TILE), single=True)],
        out_specs=pl.BlockSpec((1, HEADS * MLA_V, TILE), lambda i: (i, 0, 0)),
        scratch_shapes=_attn_scratch(2 * LANES),
        compiler_params=_params(1),
        name="mla_attn",
    )(qt, kn.reshape(nb, TILE, -1), kr.reshape(nb, TILE, LANES), vt)


def _moba_attn(slopes, qt, k, vt, sel, pos_row, pos_rep):
    nb = qt.shape[0]
    return pl.pallas_call(
        _moba_kernel,
        out_shape=jax.ShapeDtypeStruct((nb, HEADS * MOBA_DIM, TILE), BF16),
        grid=(nb,),
        in_specs=[pl.BlockSpec(memory_space=pltpu.SMEM),
                  pl.BlockSpec((1, HEADS * MOBA_DIM, TILE), lambda i: (i, 0, 0)),
                  _const_spec((nb, TILE, HEADS * MOBA_DIM), single=True),
                  _const_spec((nb, HEADS * MOBA_DIM, TILE), single=True),
                  pl.BlockSpec((None, HEADS, nb, TILE), lambda i: (i, 0, 0, 0)),
                  pl.BlockSpec((1, TILE), lambda i: (0, i)),
                  _const_spec((nb, TILE, LANES), single=True)],
        out_specs=pl.BlockSpec((1, HEADS * MOBA_DIM, TILE), lambda i: (i, 0, 0)),
        scratch_shapes=_attn_scratch(LANES),
        compiler_params=_params(1),
        name="moba_attn",
    )(slopes, qt, k.reshape(nb, TILE, -1), vt, sel, pos_row, pos_rep.reshape(nb, TILE, LANES))


def kernel(x, positions, ffn1_norm, ffn1_w_gate, ffn1_w_up, ffn1_w_down, mix_norm, w_in, mla_q_norm,
           mla_w_uq, mla_kv_norm, mla_w_ukv, w_out, ffn2_norm, ffn2_w_gate, ffn2_w_up, ffn2_w_down,
           final_norm):
    b, s, d = x.shape
    assert b == 1 and d == D_MODEL and s % FFN_ROWS == 0 and w_in.shape[0] == 1
    xs = x[0]
    pos_row = positions.reshape(1, s)
    pos_f = positions.reshape(s, 1).astype(F32)
    pos_rep = jnp.broadcast_to(pos_f, (s, LANES))
    inv = jnp.power(ROPE_BASE, -jnp.arange(ROPE_HALF, dtype=F32) * 2.0 / MLA_ROPE).reshape(ROPE_HALF, 1)
    slopes = jnp.exp2(-8.0 * jnp.arange(1, HEADS + 1, dtype=F32) / HEADS)

    wi = w_in[0]
    c_kr = MLA_Q_RANK + MLA_KV_RANK
    win = jnp.concatenate(
        [wi[:, :c_kr], wi[:, c_kr + MLA_ROPE:], wi[:, c_kr:c_kr + MLA_ROPE],
         jnp.zeros((D_MODEL, LANES - MLA_ROPE), F32)], axis=1).astype(BF16)
    wq3 = mla_w_uq[0].reshape(MLA_Q_RANK, HEADS, MLA_NOPE + MLA_ROPE)
    wuq = jnp.concatenate([wq3[:, :, :MLA_NOPE].reshape(MLA_Q_RANK, -1),
                           wq3[:, :, MLA_NOPE:].reshape(MLA_Q_RANK, -1)], axis=1).astype(BF16)
    wkv3 = mla_w_ukv[0].reshape(MLA_KV_RANK, HEADS, MLA_NOPE + MLA_V)
    wk = wkv3[:, :, :MLA_NOPE].reshape(MLA_KV_RANK, -1).astype(BF16)
    wv = wkv3[:, :, MLA_NOPE:].reshape(MLA_KV_RANK, -1).astype(BF16)
    row = lambda g: g.reshape(1, -1)

    x1 = _ffn1(xs, row(ffn1_norm[0]), ffn1_w_gate[0].astype(BF16), ffn1_w_up[0].astype(BF16),
               ffn1_w_down[0].astype(BF16))
    qa, ka, kr, va, qb, kb, vb, sel = _proj(x1, pos_row, inv, row(mix_norm[0]), win,
                                            row(mla_q_norm[0]), wuq, row(mla_kv_norm[0]), wk, wv)
    at = _mla_attn(qa, ka, kr, va)
    bt = _moba_attn(slopes, qb, kb, vb, sel, pos_row.astype(F32), pos_rep)
    out = _mix_ffn2(x1, at, bt, w_out[0].astype(BF16), row(ffn2_norm[0]), ffn2_w_gate[0].astype(BF16),
                    ffn2_w_up[0].astype(BF16), ffn2_w_down[0].astype(BF16), row(final_norm))
    return out[None]
```

```python
import math

import jax
import jax.numpy as jnp
from jax import lax
from jax.experimental import pallas as pl
from jax.experimental.pallas import tpu as pltpu

F32 = jnp.float32
BF16 = jnp.bfloat16

D_MODEL = 1024
D_FF = 2816
HEADS = 8
MLA_NOPE = 64
MLA_ROPE = 32
MLA_V = 64
MLA_Q_RANK = 384
MLA_KV_RANK = 256
ROPE_BASE = 10000.0
ROPE_HALF = MLA_ROPE // 2
MOBA_DIM = 64
MOBA_BLOCK = 256
MOBA_TOPK = 3
EPS = 1e-6
NEG_INF = -1e30
LOG2E = math.log2(math.e)

LANES = 128
TILE = MOBA_BLOCK
FFN_ROWS = 512
FFN_CHUNK = 256
VMEM_LIMIT = 56 * 1024 * 1024
LOOKAHEAD = 4
CARRY = 4
HEAD_W = 64
ONES_ROWS = 16
ACC_ROWS = HEAD_W + ONES_ROWS
AUX_ROWS = 16
AUX_MASK_ROW = 6
POS_SPLIT = 128
POS_LIMIT = POS_SPLIT * 256
assert MLA_NOPE == HEAD_W and MOBA_DIM == HEAD_W and MLA_V == HEAD_W and LOOKAHEAD < HEADS

C_CQ = 0
C_CKV = C_CQ + MLA_Q_RANK
C_MQ = C_CKV + MLA_KV_RANK
C_MK = C_MQ + HEADS * MOBA_DIM
C_MV = C_MK + HEADS * MOBA_DIM
C_KR = C_MV + HEADS * MOBA_DIM
D_IN_PAD = C_KR + LANES


def _rms(x, g):
    y = x * lax.rsqrt(jnp.mean(x * x, axis=-1, keepdims=True) + EPS)
    return y * g


def _swiglu(h, wg_ref, wu_ref, wd_ref):
    acc = None
    for c in range(D_FF // FFN_CHUNK):
        sl = slice(c * FFN_CHUNK, (c + 1) * FFN_CHUNK)
        g = jnp.dot(h, wg_ref[:, sl], preferred_element_type=F32)
        u = jnp.dot(h, wu_ref[:, sl], preferred_element_type=F32)
        a = (g * jax.nn.sigmoid(g) * u).astype(BF16)
        d = jnp.dot(a, wd_ref[sl, :], preferred_element_type=F32)
        acc = d if acc is None else acc + d
    return acc


def _ffn1_kernel(x_ref, g_ref, wg_ref, wu_ref, wd_ref, o_ref):
    x = x_ref[...]
    h = _rms(x, g_ref[...]).astype(BF16)
    o_ref[...] = x + 0.5 * _swiglu(h, wg_ref, wu_ref, wd_ref)


def _mix_ffn2_kernel(x_ref, at_ref, bt_ref, wo_ref, g_ref, wg_ref, wu_ref, wd_ref, gf_ref, o_ref):
    half = HEADS * MLA_V
    tn = (((0,), (0,)), ((), ()))
    parts = []
    for j in range(FFN_ROWS // TILE):
        pa = lax.dot_general(at_ref[j], wo_ref[0:half, :], tn, preferred_element_type=F32)
        pb = lax.dot_general(bt_ref[j], wo_ref[half:2 * half, :], tn, preferred_element_type=F32)
        parts.append(pa + pb)
    x = x_ref[...] + jnp.concatenate(parts, axis=0)
    h = _rms(x, g_ref[...]).astype(BF16)
    y = x + 0.5 * _swiglu(h, wg_ref, wu_ref, wd_ref)
    o_ref[...] = _rms(y, gf_ref[...])


def _proj_kernel(x_ref, pos_ref, inv_ref, gm_ref, win_ref, gq_ref, wuq_ref, gkv_ref, wk_ref, wv_ref,
                 qa_ref, ka_ref, kr_ref, va_ref, qb_ref, kb_ref, vb_ref, sel_ref, km_ref):
    i = pl.program_id(0)
    nb = km_ref.shape[0]

    @pl.when(i == 0)
    def _():
        km_ref[...] = jnp.zeros_like(km_ref)

    h = _rms(x_ref[...], gm_ref[...]).astype(BF16)
    u = jnp.dot(h, win_ref[...], preferred_element_type=F32)

    ang = inv_ref[...] * pos_ref[...].astype(F32)
    cos, sin = jnp.cos(ang), jnp.sin(ang)

    def rope_t(xt):
        x1, x2 = xt[:ROPE_HALF], xt[ROPE_HALF:]
        return x1 * cos - x2 * sin, x1 * sin + x2 * cos

    cqn = _rms(u[:, C_CQ:C_CQ + MLA_Q_RANK], gq_ref[...]).astype(BF16)
    q = jnp.dot(cqn, wuq_ref[...], preferred_element_type=F32)
    qt = q.T * ((MLA_NOPE + MLA_ROPE) ** -0.5 * LOG2E)
    n_nope = HEADS * MLA_NOPE
    qa_ref[0, 0:n_nope, :] = qt[:n_nope].astype(BF16)
    for hd in range(HEADS):
        r0 = n_nope + hd * MLA_ROPE
        r1, r2 = rope_t(qt[r0:r0 + MLA_ROPE])
        qa_ref[0, r0:r0 + ROPE_HALF, :] = r1.astype(BF16)
        qa_ref[0, r0 + ROPE_HALF:r0 + MLA_ROPE, :] = r2.astype(BF16)

    ckvn = _rms(u[:, C_CKV:C_CKV + MLA_KV_RANK], gkv_ref[...]).astype(BF16)
    ka_ref[...] = jnp.dot(ckvn, wk_ref[...], preferred_element_type=F32).astype(BF16)
    v = jnp.dot(ckvn, wv_ref[...], preferred_element_type=F32)
    krt = u[:, C_KR:C_KR + LANES].T
    k1, k2 = rope_t(krt[:MLA_ROPE])
    kr_ref[...] = jnp.concatenate(
        [k1, k2, jnp.zeros((LANES - MLA_ROPE, TILE), F32)], axis=0).T.astype(BF16)
    va_ref[0] = v.T.astype(BF16)

    mq = u[:, C_MQ:C_MQ + HEADS * MOBA_DIM]
    mk = u[:, C_MK:C_MK + HEADS * MOBA_DIM]
    mv = u[:, C_MV:C_MV + HEADS * MOBA_DIM]
    mqt = mq.T
    qb_ref[0] = (mqt * (MOBA_DIM ** -0.5 * LOG2E)).astype(BF16)
    kb_ref[...] = mk.astype(BF16)
    vb_ref[0] = mv.T.astype(BF16)

    row = lax.broadcasted_iota(jnp.int32, (nb, TILE), 0)
    for hd in range(HEADS):
        gate = jnp.dot(km_ref[:, hd * MOBA_DIM:(hd + 1) * MOBA_DIM],
                       mqt[hd * MOBA_DIM:(hd + 1) * MOBA_DIM],
                       preferred_element_type=F32, precision=lax.Precision.HIGHEST)
        g = jnp.where(row < i, gate, NEG_INF)
        chosen = row == i
        for r in range(MOBA_TOPK):
            mx = jnp.max(g, axis=0, keepdims=True)
            idx = jnp.min(jnp.where(g == mx, row, nb), axis=0, keepdims=True)
            hit = row == idx
            chosen = jnp.logical_or(chosen, jnp.logical_and(hit, i > r))
            g = jnp.where(hit, -jnp.inf, g)
        sel_ref[hd] = jnp.where(chosen, 0.0, NEG_INF).astype(F32)
    km_ref[pl.ds(i, 1), :] = jnp.mean(mk, axis=0, keepdims=True)


def _online_update(hd, s, vt, m_ref, acc_ref):
    m_old = m_ref[hd]
    m_new = jnp.maximum(m_old, jnp.max(s, axis=0, keepdims=True))
    p = jnp.exp2(s - m_new).astype(BF16)
    alpha = jnp.exp2(m_old - m_new)
    vt1 = jnp.concatenate([vt, jnp.ones((ONES_ROWS, TILE), BF16)], axis=0)
    acc_ref[hd] = alpha * acc_ref[hd] + jnp.dot(vt1, p, preferred_element_type=F32)
    m_ref[hd] = m_new


def _init_state(m_ref, acc_ref):
    m_ref[...] = jnp.full(m_ref.shape, NEG_INF, F32)
    acc_ref[...] = jnp.zeros(acc_ref.shape, F32)


def _write_out(o_ref, acc_ref):
    for hd in range(HEADS):
        acc = acc_ref[hd]
        o_ref[0, hd * HEAD_W:(hd + 1) * HEAD_W, :] = (acc[:HEAD_W] / acc[HEAD_W:HEAD_W + 1]).astype(o_ref.dtype)


def _pair_pad(x, hd):
    z = jnp.zeros_like(x)
    return jnp.concatenate([x, z] if hd % 2 == 0 else [z, x], axis=0)


def _pipelined_tile(scores_now, scores_next, finish, s_ref):
    fresh = {hd: scores_now(hd) for hd in range(CARRY, LOOKAHEAD)}
    for t in range(HEADS):
        ahead = t + LOOKAHEAD
        if ahead < HEADS:
            fresh[ahead] = scores_now(ahead)
        elif ahead - HEADS < CARRY and scores_next is not None:
            s_ref[ahead - HEADS] = scores_next(ahead - HEADS)
        finish(t, s_ref[t] if t < CARRY else fresh.pop(t))


def _for_past_tiles(count, body):
    def pair(i, c):
        body(2 * i)
        body(2 * i + 1)
        return c

    lax.fori_loop(0, count // 2, pair, 0)

    @pl.when(count % 2 == 1)
    def _():
        body(count - 1)


def _causal_mask():
    key = lax.broadcasted_iota(jnp.int32, (TILE, TILE), 0)
    qry = lax.broadcasted_iota(jnp.int32, (TILE, TILE), 1)
    return key <= qry


def _mla_kernel(qt_ref, kn_ref, kr_ref, vt_ref, o_ref, qp_ref, m_ref, acc_ref, s_ref):
    qi = pl.program_id(0)
    n_nope = HEADS * MLA_NOPE
    zrope = jnp.zeros((LANES - MLA_ROPE, TILE), BF16)
    for hd in range(HEADS):
        nope = qt_ref[0, hd * MLA_NOPE:(hd + 1) * MLA_NOPE, :]
        rope = qt_ref[0, n_nope + hd * MLA_ROPE:n_nope + (hd + 1) * MLA_ROPE, :]
        qp_ref[hd] = jnp.concatenate([_pair_pad(nope, hd), rope, zrope], axis=0)
    _init_state(m_ref, acc_ref)

    def scores(n):
        def f(hd):
            j = hd // 2
            kcat = jnp.concatenate([kn_ref[n, :, j * LANES:(j + 1) * LANES], kr_ref[n]], axis=1)
            return jnp.dot(kcat, qp_ref[hd], preferred_element_type=F32)
        return f

    def finish(n, mask):
        def f(hd, s):
            if mask is not None:
                s = jnp.where(mask, s, NEG_INF)
            _online_update(hd, s, vt_ref[n, hd * HEAD_W:(hd + 1) * HEAD_W, :], m_ref, acc_ref)
        return f

    for hd in range(CARRY):
        s_ref[hd] = scores(0)(hd)

    def past(n):
        _pipelined_tile(scores(n), scores(n + 1), finish(n, None), s_ref)

    _for_past_tiles(qi, past)
    _pipelined_tile(scores(qi), None, finish(qi, _causal_mask()), s_ref)
    _write_out(o_ref, acc_ref)


def _moba_kernel(c_ref, fast_ref, base_ref, qt_ref, k_ref, vt_ref, sel_ref, pq_ref, pk_ref, crow_ref, o_ref,
                 qp_ref, m_ref, acc_ref, s_ref):
    qi = pl.program_id(0)
    for hd in range(HEADS):
        qp_ref[hd] = _pair_pad(qt_ref[0, hd * MOBA_DIM:(hd + 1) * MOBA_DIM, :], hd)
    _init_state(m_ref, acc_ref)
    base = base_ref[qi]
    pq = pq_ref[...] - base
    row16 = lax.broadcasted_iota(jnp.int32, (AUX_ROWS, TILE), 0)
    lane = lax.broadcasted_iota(jnp.int32, (TILE, LANES), 1)
    zrows = jnp.zeros((LANES - AUX_ROWS, TILE), BF16)

    def kpair(n, hd):
        j = hd // 2
        return k_ref[n, :, j * LANES:(j + 1) * LANES]

    def scores_plain(n):
        def f(hd):
            return jnp.dot(kpair(n, hd), qp_ref[hd], preferred_element_type=F32)
        return f

    def scores_linear(n):
        rk = pk_ref[n] - base
        hi = jnp.floor(rk * (1.0 / POS_SPLIT))
        lo = rk - POS_SPLIT * hi
        aux = jnp.where(lane < 3, hi, jnp.where(lane < 6, lo, jnp.where(lane == 6, 1.0, 0.0))).astype(BF16)

        def f(hd):
            blk = jnp.where(row16 == AUX_MASK_ROW, sel_ref[hd, pl.ds(n, 1), :], crow_ref[hd])
            rhs = jnp.concatenate([qp_ref[hd], blk.astype(BF16), zrows], axis=0)
            lhs = jnp.concatenate([kpair(n, hd), aux], axis=1)
            return jnp.dot(lhs, rhs, preferred_element_type=F32)
        return f

    def finish_general(n, own):
        pk = pk_ref[n] - base
        dist = jnp.abs(jnp.concatenate([pk] * (TILE // LANES), axis=1) - pq)
        mask = _causal_mask() if own else None

        def f(hd, s):
            s = s - c_ref[hd] * dist + c_ref[hd] * pq
            if own:
                s = jnp.where(mask, s, NEG_INF)
            else:
                s = s + sel_ref[hd, pl.ds(n, 1), :]
            _online_update(hd, s, vt_ref[n, hd * HEAD_W:(hd + 1) * HEAD_W, :], m_ref, acc_ref)
        return f

    def finish_linear(n):
        def f(hd, s):
            _online_update(hd, s, vt_ref[n, hd * HEAD_W:(hd + 1) * HEAD_W, :], m_ref, acc_ref)
        return f

    for hd in range(CARRY):
        s_ref[hd] = scores_plain(qi)(hd)
    _pipelined_tile(scores_plain(qi), None, finish_general(qi, True), s_ref)

    @pl.when(fast_ref[qi] == 1)
    def _():
        for hd in range(CARRY):
            s_ref[hd] = scores_linear(0)(hd)
        _for_past_tiles(qi, lambda n: _pipelined_tile(scores_linear(n), scores_linear(n + 1),
                                                      finish_linear(n), s_ref))

    @pl.when(fast_ref[qi] != 1)
    def _():
        for hd in range(CARRY):
            s_ref[hd] = scores_plain(0)(hd)

        def past(n, c):
            _pipelined_tile(scores_plain(n), scores_plain(n + 1), finish_general(n, False), s_ref)
            return c

        lax.fori_loop(0, qi, past, 0)

    _write_out(o_ref, acc_ref)


def _const_spec(shape, single=False):
    zeros = (0,) * len(shape)
    if single:
        return pl.BlockSpec(shape, lambda *_: zeros, pipeline_mode=pl.Buffered(1))
    return pl.BlockSpec(shape, lambda *_: zeros)


def _params(n_axes):
    return pltpu.CompilerParams(dimension_semantics=("arbitrary",) * n_axes,
                                vmem_limit_bytes=VMEM_LIMIT)


def _ffn1(x, g, wg, wu, wd):
    s = x.shape[0]
    row = pl.BlockSpec((FFN_ROWS, D_MODEL), lambda i: (i, 0))
    return pl.pallas_call(
        _ffn1_kernel,
        out_shape=jax.ShapeDtypeStruct((s, D_MODEL), F32),
        grid=(s // FFN_ROWS,),
        in_specs=[row, _const_spec((1, D_MODEL)), _const_spec((D_MODEL, D_FF)),
                  _const_spec((D_MODEL, D_FF)), _const_spec((D_FF, D_MODEL))],
        out_specs=row,
        compiler_params=_params(1),
        name="ffn1",
    )(x, g, wg, wu, wd)


def _mix_ffn2(x, at, bt, wo, g, wg, wu, wd, gf):
    s = x.shape[0]
    row = pl.BlockSpec((FFN_ROWS, D_MODEL), lambda i: (i, 0))
    headt = pl.BlockSpec((FFN_ROWS // TILE, HEADS * MLA_V, TILE), lambda i: (i, 0, 0))
    return pl.pallas_call(
        _mix_ffn2_kernel,
        out_shape=jax.ShapeDtypeStruct((s, D_MODEL), F32),
        grid=(s // FFN_ROWS,),
        in_specs=[row, headt, headt, _const_spec((D_MODEL, D_MODEL)), _const_spec((1, D_MODEL)),
                  _const_spec((D_MODEL, D_FF)), _const_spec((D_MODEL, D_FF)),
                  _const_spec((D_FF, D_MODEL)), _const_spec((1, D_MODEL))],
        out_specs=row,
        compiler_params=_params(1),
        name="mix_ffn2",
    )(x, at, bt, wo, g, wg, wu, wd, gf)


def _proj(x, pos_row, inv, gm, win, gq, wuq, gkv, wk, wv):
    s = x.shape[0]
    nb = s // TILE
    tok_t = lambda rows: pl.BlockSpec((1, rows, TILE), lambda i: (i, 0, 0))
    tok = lambda cols: pl.BlockSpec((TILE, cols), lambda i: (i, 0))
    n_q = HEADS * (MLA_NOPE + MLA_ROPE)
    out_shape = (
        jax.ShapeDtypeStruct((nb, n_q, TILE), BF16),
        jax.ShapeDtypeStruct((s, HEADS * MLA_NOPE), BF16),
        jax.ShapeDtypeStruct((s, LANES), BF16),
        jax.ShapeDtypeStruct((nb, HEADS * MLA_V, TILE), BF16),
        jax.ShapeDtypeStruct((nb, HEADS * MOBA_DIM, TILE), BF16),
        jax.ShapeDtypeStruct((s, HEADS * MOBA_DIM), BF16),
        jax.ShapeDtypeStruct((nb, HEADS * MOBA_DIM, TILE), BF16),
        jax.ShapeDtypeStruct((nb, HEADS, nb, TILE), F32),
    )
    out_specs = (
        tok_t(n_q), tok(HEADS * MLA_NOPE), tok(LANES), tok_t(HEADS * MLA_V),
        tok_t(HEADS * MOBA_DIM), tok(HEADS * MOBA_DIM), tok_t(HEADS * MOBA_DIM),
        pl.BlockSpec((None, HEADS, nb, TILE), lambda i: (i, 0, 0, 0)),
    )
    in_specs = [
        tok(D_MODEL), pl.BlockSpec((1, TILE), lambda i: (0, i)), _const_spec((ROPE_HALF, 1)),
        _const_spec((1, D_MODEL)), _const_spec((D_MODEL, D_IN_PAD)),
        _const_spec((1, MLA_Q_RANK)), _const_spec((MLA_Q_RANK, n_q)),
        _const_spec((1, MLA_KV_RANK)), _const_spec((MLA_KV_RANK, HEADS * MLA_NOPE)),
        _const_spec((MLA_KV_RANK, HEADS * MLA_V)),
    ]
    return pl.pallas_call(
        _proj_kernel,
        out_shape=out_shape,
        grid=(nb,),
        in_specs=in_specs,
        out_specs=out_specs,
        scratch_shapes=[pltpu.VMEM((nb, HEADS * MOBA_DIM), F32)],
        compiler_params=_params(1),
        name="proj",
    )(x, pos_row, inv, gm, win, gq, wuq, gkv, wk, wv)


def _attn_scratch(q_rows):
    return [pltpu.VMEM((HEADS, q_rows, TILE), BF16), pltpu.VMEM((HEADS, 1, TILE), F32),
            pltpu.VMEM((HEADS, ACC_ROWS, TILE), F32), pltpu.VMEM((CARRY, TILE, TILE), F32)]


def _mla_attn(qt, kn, kr, vt):
    nb = qt.shape[0]
    n_q = HEADS * (MLA_NOPE + MLA_ROPE)
    return pl.pallas_call(
        _mla_kernel,
        out_shape=jax.ShapeDtypeStruct((nb, HEADS * MLA_V, TILE), BF16),
        grid=(nb,),
        in_specs=[pl.BlockSpec((1, n_q, TILE), lambda i: (i, 0, 0)),
                  _const_spec((nb, TILE, HEADS * MLA_NOPE), single=True),
                  _const_spec((nb, TILE, LANES), single=True),
                  _const_spec((nb, HEADS * MLA_V, TILE), single=True)],
        out_specs=pl.BlockSpec((1, HEADS * MLA_V, TILE), lambda i: (i, 0, 0)),
        scratch_shapes=_attn_scratch(2 * LANES),
        compiler_params=_params(1),
        name="mla_attn",
    )(qt, kn.reshape(nb, TILE, -1), kr.reshape(nb, TILE, LANES), vt)


def _moba_attn(c, fast, base, qt, k, vt, sel, pos_row, pos_rep, crow):
    nb = qt.shape[0]
    smem = pl.BlockSpec(memory_space=pltpu.SMEM)
    return pl.pallas_call(
        _moba_kernel,
        out_shape=jax.ShapeDtypeStruct((nb, HEADS * MOBA_DIM, TILE), BF16),
        grid=(nb,),
        in_specs=[smem, smem, smem,
                  pl.BlockSpec((1, HEADS * MOBA_DIM, TILE), lambda i: (i, 0, 0)),
                  _const_spec((nb, TILE, HEADS * MOBA_DIM), single=True),
                  _const_spec((nb, HEADS * MOBA_DIM, TILE), single=True),
                  pl.BlockSpec((None, HEADS, nb, TILE), lambda i: (i, 0, 0, 0)),
                  pl.BlockSpec((1, TILE), lambda i: (0, i)),
                  _const_spec((nb, TILE, LANES), single=True),
                  _const_spec((HEADS, AUX_ROWS, TILE), single=True)],
        out_specs=pl.BlockSpec((1, HEADS * MOBA_DIM, TILE), lambda i: (i, 0, 0)),
        scratch_shapes=_attn_scratch(LANES),
        compiler_params=_params(1),
        name="moba_attn",
    )(c, fast, base, qt, k.reshape(nb, TILE, -1), vt, sel, pos_row, pos_rep.reshape(nb, TILE, LANES), crow)


def _alibi_tables(positions, s):
    nb = s // TILE
    pos = positions.reshape(s)
    c = jnp.exp2(-8.0 * jnp.arange(1, HEADS + 1, dtype=F32) / HEADS) * LOG2E
    c1 = c.astype(BF16).astype(F32)
    c2 = (c - c1).astype(BF16).astype(F32)
    c3 = (c - c1 - c2).astype(BF16).astype(F32)
    rows = jnp.stack([POS_SPLIT * c1, POS_SPLIT * c2, POS_SPLIT * c3, c1, c2, c3], axis=1)
    crow = jnp.pad(rows, ((0, 0), (0, AUX_ROWS - rows.shape[1])))
    crow = jnp.broadcast_to(crow[:, :, None], (HEADS, AUX_ROWS, TILE))
    blk = pos.reshape(nb, TILE)
    base = blk.min(axis=1)
    past_max = jnp.concatenate([base[:1], lax.cummax(blk.max(axis=1))[:-1]])
    in_range = (pos.max() - pos.min()) < POS_LIMIT
    fast = jnp.logical_and(base >= past_max, in_range).astype(jnp.int32)
    return c, fast, base.astype(F32), crow


def kernel(x, positions, ffn1_norm, ffn1_w_gate, ffn1_w_up, ffn1_w_down, mix_norm, w_in, mla_q_norm,
           mla_w_uq, mla_kv_norm, mla_w_ukv, w_out, ffn2_norm, ffn2_w_gate, ffn2_w_up, ffn2_w_down,
           final_norm):
    b, s, d = x.shape
    assert b == 1 and d == D_MODEL and s % FFN_ROWS == 0 and w_in.shape[0] == 1
    xs = x[0]
    pos_row = positions.reshape(1, s)
    pos_f = positions.reshape(s, 1).astype(F32)
    pos_rep = jnp.broadcast_to(pos_f, (s, LANES))
    inv = jnp.power(ROPE_BASE, -jnp.arange(ROPE_HALF, dtype=F32) * 2.0 / MLA_ROPE).reshape(ROPE_HALF, 1)
    c_alibi, fast, base, crow = _alibi_tables(positions, s)

    wi = w_in[0]
    c_kr = MLA_Q_RANK + MLA_KV_RANK
    win = jnp.concatenate(
        [wi[:, :c_kr], wi[:, c_kr + MLA_ROPE:], wi[:, c_kr:c_kr + MLA_ROPE],
         jnp.zeros((D_MODEL, LANES - MLA_ROPE), F32)], axis=1).astype(BF16)
    wq3 = mla_w_uq[0].reshape(MLA_Q_RANK, HEADS, MLA_NOPE + MLA_ROPE)
    wuq = jnp.concatenate([wq3[:, :, :MLA_NOPE].reshape(MLA_Q_RANK, -1),
                           wq3[:, :, MLA_NOPE:].reshape(MLA_Q_RANK, -1)], axis=1).astype(BF16)
    wkv3 = mla_w_ukv[0].reshape(MLA_KV_RANK, HEADS, MLA_NOPE + MLA_V)
    wk = wkv3[:, :, :MLA_NOPE].reshape(MLA_KV_RANK, -1).astype(BF16)
    wv = wkv3[:, :, MLA_NOPE:].reshape(MLA_KV_RANK, -1).astype(BF16)
    row = lambda g: g.reshape(1, -1)

    x1 = _ffn1(xs, row(ffn1_norm[0]), ffn1_w_gate[0].astype(BF16), ffn1_w_up[0].astype(BF16),
               ffn1_w_down[0].astype(BF16))
    qa, ka, kr, va, qb, kb, vb, sel = _proj(x1, pos_row, inv, row(mix_norm[0]), win,
                                            row(mla_q_norm[0]), wuq, row(mla_kv_norm[0]), wk, wv)
    at = _mla_attn(qa, ka, kr, va)
    bt = _moba_attn(c_alibi, fast, base, qb, kb, vb, sel, pos_row.astype(F32), pos_rep, crow)
    out = _mix_ffn2(x1, at, bt, w_out[0].astype(BF16), row(ffn2_norm[0]), ffn2_w_gate[0].astype(BF16),
                    ffn2_w_up[0].astype(BF16), ffn2_w_down[0].astype(BF16), row(final_norm))
    return out[None]
```

```python
import math

import jax
import jax.numpy as jnp
from jax import lax
from jax.experimental import pallas as pl
from jax.experimental.pallas import tpu as pltpu

F32 = jnp.float32
BF16 = jnp.bfloat16

D_MODEL = 1024
D_FF = 2816
HEADS = 8
MLA_NOPE = 64
MLA_ROPE = 32
MLA_V = 64
MLA_Q_RANK = 384
MLA_KV_RANK = 256
ROPE_BASE = 10000.0
ROPE_HALF = MLA_ROPE // 2
MOBA_DIM = 64
MOBA_BLOCK = 256
MOBA_TOPK = 3
EPS = 1e-6
NEG_INF = -1e30
LOG2E = math.log2(math.e)

LANES = 128
TILE = MOBA_BLOCK
FFN_ROWS = 512
FFN_CHUNK = 256
VMEM_LIMIT = 56 * 1024 * 1024
LOOKAHEAD = 5
CARRY = 5
PAST_UNROLL = 4
HEAD_W = 64
ONES_ROWS = 16
ACC_ROWS = HEAD_W + ONES_ROWS
AUX_ROWS = 16
AUX_MASK_ROW = 6
POS_SPLIT = 128
POS_LIMIT = POS_SPLIT * 256
assert MLA_NOPE == HEAD_W and MOBA_DIM == HEAD_W and MLA_V == HEAD_W and LOOKAHEAD < HEADS

C_CQ = 0
C_CKV = C_CQ + MLA_Q_RANK
C_MQ = C_CKV + MLA_KV_RANK
C_MK = C_MQ + HEADS * MOBA_DIM
C_MV = C_MK + HEADS * MOBA_DIM
C_KR = C_MV + HEADS * MOBA_DIM
D_IN_PAD = C_KR + LANES


def _rms(x, g):
    y = x * lax.rsqrt(jnp.mean(x * x, axis=-1, keepdims=True) + EPS)
    return y * g


def _swiglu(h, wg_ref, wu_ref, wd_ref):
    acc = None
    for c in range(D_FF // FFN_CHUNK):
        sl = slice(c * FFN_CHUNK, (c + 1) * FFN_CHUNK)
        g = jnp.dot(h, wg_ref[:, sl], preferred_element_type=F32)
        u = jnp.dot(h, wu_ref[:, sl], preferred_element_type=F32)
        a = (g * jax.nn.sigmoid(g) * u).astype(BF16)
        d = jnp.dot(a, wd_ref[sl, :], preferred_element_type=F32)
        acc = d if acc is None else acc + d
    return acc


def _ffn1_kernel(x_ref, g_ref, wg_ref, wu_ref, wd_ref, o_ref):
    x = x_ref[...]
    h = _rms(x, g_ref[...]).astype(BF16)
    o_ref[...] = x + 0.5 * _swiglu(h, wg_ref, wu_ref, wd_ref)


def _mix_ffn2_kernel(x_ref, at_ref, bt_ref, wo_ref, g_ref, wg_ref, wu_ref, wd_ref, gf_ref, o_ref):
    half = HEADS * MLA_V
    tn = (((0,), (0,)), ((), ()))
    parts = []
    for j in range(FFN_ROWS // TILE):
        pa = lax.dot_general(at_ref[j], wo_ref[0:half, :], tn, preferred_element_type=F32)
        pb = lax.dot_general(bt_ref[j], wo_ref[half:2 * half, :], tn, preferred_element_type=F32)
        parts.append(pa + pb)
    x = x_ref[...] + jnp.concatenate(parts, axis=0)
    h = _rms(x, g_ref[...]).astype(BF16)
    y = x + 0.5 * _swiglu(h, wg_ref, wu_ref, wd_ref)
    o_ref[...] = _rms(y, gf_ref[...])


def _proj_kernel(x_ref, pos_ref, inv_ref, gm_ref, win_ref, gq_ref, wuq_ref, gkv_ref, wk_ref, wv_ref,
                 qa_ref, ka_ref, kr_ref, va_ref, qb_ref, kb_ref, vb_ref, sel_ref, km_ref):
    i = pl.program_id(0)
    nb = km_ref.shape[0]

    @pl.when(i == 0)
    def _():
        km_ref[...] = jnp.zeros_like(km_ref)

    h = _rms(x_ref[...], gm_ref[...]).astype(BF16)
    u = jnp.dot(h, win_ref[...], preferred_element_type=F32)

    ang = inv_ref[...] * pos_ref[...].astype(F32)
    cos, sin = jnp.cos(ang), jnp.sin(ang)

    def rope_t(xt):
        x1, x2 = xt[:ROPE_HALF], xt[ROPE_HALF:]
        return x1 * cos - x2 * sin, x1 * sin + x2 * cos

    cqn = _rms(u[:, C_CQ:C_CQ + MLA_Q_RANK], gq_ref[...]).astype(BF16)
    q = jnp.dot(cqn, wuq_ref[...], preferred_element_type=F32)
    qt = q.T * ((MLA_NOPE + MLA_ROPE) ** -0.5 * LOG2E)
    n_nope = HEADS * MLA_NOPE
    qa_ref[0, 0:n_nope, :] = qt[:n_nope].astype(BF16)
    for hd in range(HEADS):
        r0 = n_nope + hd * MLA_ROPE
        r1, r2 = rope_t(qt[r0:r0 + MLA_ROPE])
        qa_ref[0, r0:r0 + ROPE_HALF, :] = r1.astype(BF16)
        qa_ref[0, r0 + ROPE_HALF:r0 + MLA_ROPE, :] = r2.astype(BF16)

    ckvn = _rms(u[:, C_CKV:C_CKV + MLA_KV_RANK], gkv_ref[...]).astype(BF16)
    ka_ref[...] = jnp.dot(ckvn, wk_ref[...], preferred_element_type=F32).astype(BF16)
    v = jnp.dot(ckvn, wv_ref[...], preferred_element_type=F32)
    krt = u[:, C_KR:C_KR + LANES].T
    k1, k2 = rope_t(krt[:MLA_ROPE])
    kr_ref[...] = jnp.concatenate(
        [k1, k2, jnp.zeros((LANES - MLA_ROPE, TILE), F32)], axis=0).T.astype(BF16)
    va_ref[0] = v.T.astype(BF16)

    mq = u[:, C_MQ:C_MQ + HEADS * MOBA_DIM]
    mk = u[:, C_MK:C_MK + HEADS * MOBA_DIM]
    mv = u[:, C_MV:C_MV + HEADS * MOBA_DIM]
    mqt = mq.T
    qb_ref[0] = (mqt * (MOBA_DIM ** -0.5 * LOG2E)).astype(BF16)
    kb_ref[...] = mk.astype(BF16)
    vb_ref[0] = mv.T.astype(BF16)

    row = lax.broadcasted_iota(jnp.int32, (nb, TILE), 0)
    for hd in range(HEADS):
        gate = jnp.dot(km_ref[:, hd * MOBA_DIM:(hd + 1) * MOBA_DIM],
                       mqt[hd * MOBA_DIM:(hd + 1) * MOBA_DIM],
                       preferred_element_type=F32, precision=lax.Precision.HIGHEST)
        g = jnp.where(row < i, gate, NEG_INF)
        chosen = row == i
        for r in range(MOBA_TOPK):
            mx = jnp.max(g, axis=0, keepdims=True)
            idx = jnp.min(jnp.where(g == mx, row, nb), axis=0, keepdims=True)
            hit = row == idx
            chosen = jnp.logical_or(chosen, jnp.logical_and(hit, i > r))
            g = jnp.where(hit, -jnp.inf, g)
        sel_ref[hd] = jnp.where(chosen, 0.0, NEG_INF).astype(F32)
    km_ref[pl.ds(i, 1), :] = jnp.mean(mk, axis=0, keepdims=True)


def _online_update(hd, s, vt, m_ref, acc_ref):
    m_old = m_ref[hd]
    m_new = jnp.maximum(m_old, jnp.max(s, axis=0, keepdims=True))
    p = jnp.exp2(s - m_new).astype(BF16)
    alpha = jnp.exp2(m_old - m_new)
    vt1 = jnp.concatenate([vt, jnp.ones((ONES_ROWS, TILE), BF16)], axis=0)
    acc_ref[hd] = alpha * acc_ref[hd] + jnp.dot(vt1, p, preferred_element_type=F32)
    m_ref[hd] = m_new


def _init_state(m_ref, acc_ref):
    m_ref[...] = jnp.full(m_ref.shape, NEG_INF, F32)
    acc_ref[...] = jnp.zeros(acc_ref.shape, F32)


def _write_out(o_ref, acc_ref):
    for hd in range(HEADS):
        acc = acc_ref[hd]
        o_ref[0, hd * HEAD_W:(hd + 1) * HEAD_W, :] = (acc[:HEAD_W] / acc[HEAD_W:HEAD_W + 1]).astype(o_ref.dtype)


def _pair_pad(x, hd):
    z = jnp.zeros_like(x)
    return jnp.concatenate([x, z] if hd % 2 == 0 else [z, x], axis=0)


def _pipelined_tile(scores_now, scores_next, finish, s_ref):
    fresh = {hd: scores_now(hd) for hd in range(CARRY, LOOKAHEAD)}
    for t in range(HEADS):
        ahead = t + LOOKAHEAD
        if ahead < HEADS:
            fresh[ahead] = scores_now(ahead)
        elif ahead - HEADS < CARRY and scores_next is not None:
            s_ref[ahead - HEADS] = scores_next(ahead - HEADS)
        finish(t, s_ref[t] if t < CARRY else fresh.pop(t))


def _for_past_tiles(count, body):
    def group(i, c):
        for u in range(PAST_UNROLL):
            body(PAST_UNROLL * i + u)
        return c

    def single(n, c):
        body(n)
        return c

    groups = count // PAST_UNROLL
    lax.fori_loop(0, groups, group, 0)
    lax.fori_loop(groups * PAST_UNROLL, count, single, 0)


def _causal_mask():
    key = lax.broadcasted_iota(jnp.int32, (TILE, TILE), 0)
    qry = lax.broadcasted_iota(jnp.int32, (TILE, TILE), 1)
    return key <= qry


def _mla_kernel(qt_ref, kn_ref, kr_ref, vt_ref, o_ref, qp_ref, m_ref, acc_ref, s_ref):
    qi = pl.program_id(0)
    n_nope = HEADS * MLA_NOPE
    zrope = jnp.zeros((LANES - MLA_ROPE, TILE), BF16)
    for hd in range(HEADS):
        nope = qt_ref[0, hd * MLA_NOPE:(hd + 1) * MLA_NOPE, :]
        rope = qt_ref[0, n_nope + hd * MLA_ROPE:n_nope + (hd + 1) * MLA_ROPE, :]
        qp_ref[hd] = jnp.concatenate([_pair_pad(nope, hd), rope, zrope], axis=0)
    _init_state(m_ref, acc_ref)

    def scores(n):
        def f(hd):
            j = hd // 2
            kcat = jnp.concatenate([kn_ref[n, :, j * LANES:(j + 1) * LANES], kr_ref[n]], axis=1)
            return jnp.dot(kcat, qp_ref[hd], preferred_element_type=F32)
        return f

    def finish(n, mask):
        def f(hd, s):
            if mask is not None:
                s = jnp.where(mask, s, NEG_INF)
            _online_update(hd, s, vt_ref[n, hd * HEAD_W:(hd + 1) * HEAD_W, :], m_ref, acc_ref)
        return f

    for hd in range(CARRY):
        s_ref[hd] = scores(0)(hd)

    def past(n):
        _pipelined_tile(scores(n), scores(n + 1), finish(n, None), s_ref)

    _for_past_tiles(qi, past)
    _pipelined_tile(scores(qi), None, finish(qi, _causal_mask()), s_ref)
    _write_out(o_ref, acc_ref)


def _moba_kernel(c_ref, fast_ref, base_ref, qt_ref, k_ref, vt_ref, sel_ref, pq_ref, pk_ref, crow_ref, o_ref,
                 qp_ref, m_ref, acc_ref, s_ref):
    qi = pl.program_id(0)
    for hd in range(HEADS):
        qp_ref[hd] = _pair_pad(qt_ref[0, hd * MOBA_DIM:(hd + 1) * MOBA_DIM, :], hd)
    _init_state(m_ref, acc_ref)
    base = base_ref[qi]
    pq = pq_ref[...] - base
    row16 = lax.broadcasted_iota(jnp.int32, (AUX_ROWS, TILE), 0)
    lane = lax.broadcasted_iota(jnp.int32, (TILE, LANES), 1)
    zrows = jnp.zeros((LANES - AUX_ROWS, TILE), BF16)

    def kpair(n, hd):
        j = hd // 2
        return k_ref[n, :, j * LANES:(j + 1) * LANES]

    def scores_plain(n):
        def f(hd):
            return jnp.dot(kpair(n, hd), qp_ref[hd], preferred_element_type=F32)
        return f

    def scores_linear(n):
        rk = pk_ref[n] - base
        hi = jnp.floor(rk * (1.0 / POS_SPLIT))
        lo = rk - POS_SPLIT * hi
        aux = jnp.where(lane < 3, hi, jnp.where(lane < 6, lo, jnp.where(lane == 6, 1.0, 0.0))).astype(BF16)

        def f(hd):
            blk = jnp.where(row16 == AUX_MASK_ROW, sel_ref[hd, pl.ds(n, 1), :], crow_ref[hd])
            rhs = jnp.concatenate([qp_ref[hd], blk.astype(BF16), zrows], axis=0)
            lhs = jnp.concatenate([kpair(n, hd), aux], axis=1)
            return jnp.dot(lhs, rhs, preferred_element_type=F32)
        return f

    def finish_general(n, own):
        pk = pk_ref[n] - base
        dist = jnp.abs(jnp.concatenate([pk] * (TILE // LANES), axis=1) - pq)
        mask = _causal_mask() if own else None

        def f(hd, s):
            s = s - c_ref[hd] * dist + c_ref[hd] * pq
            if own:
                s = jnp.where(mask, s, NEG_INF)
            else:
                s = s + sel_ref[hd, pl.ds(n, 1), :]
            _online_update(hd, s, vt_ref[n, hd * HEAD_W:(hd + 1) * HEAD_W, :], m_ref, acc_ref)
        return f

    def finish_linear(n):
        def f(hd, s):
            _online_update(hd, s, vt_ref[n, hd * HEAD_W:(hd + 1) * HEAD_W, :], m_ref, acc_ref)
        return f

    for hd in range(CARRY):
        s_ref[hd] = scores_plain(qi)(hd)
    _pipelined_tile(scores_plain(qi), None, finish_general(qi, True), s_ref)

    @pl.when(fast_ref[qi] == 1)
    def _():
        for hd in range(CARRY):
            s_ref[hd] = scores_linear(0)(hd)
        _for_past_tiles(qi, lambda n: _pipelined_tile(scores_linear(n), scores_linear(n + 1),
                                                      finish_linear(n), s_ref))

    @pl.when(fast_ref[qi] != 1)
    def _():
        for hd in range(CARRY):
            s_ref[hd] = scores_plain(0)(hd)

        def past(n, c):
            _pipelined_tile(scores_plain(n), scores_plain(n + 1), finish_general(n, False), s_ref)
            return c

        lax.fori_loop(0, qi, past, 0)

    _write_out(o_ref, acc_ref)


def _const_spec(shape, single=False):
    zeros = (0,) * len(shape)
    if single:
        return pl.BlockSpec(shape, lambda *_: zeros, pipeline_mode=pl.Buffered(1))
    return pl.BlockSpec(shape, lambda *_: zeros)


def _params(n_axes):
    return pltpu.CompilerParams(dimension_semantics=("arbitrary",) * n_axes,
                                vmem_limit_bytes=VMEM_LIMIT)


def _ffn1(x, g, wg, wu, wd):
    s = x.shape[0]
    row = pl.BlockSpec((FFN_ROWS, D_MODEL), lambda i: (i, 0))
    return pl.pallas_call(
        _ffn1_kernel,
        out_shape=jax.ShapeDtypeStruct((s, D_MODEL), F32),
        grid=(s // FFN_ROWS,),
        in_specs=[row, _const_spec((1, D_MODEL)), _const_spec((D_MODEL, D_FF)),
                  _const_spec((D_MODEL, D_FF)), _const_spec((D_FF, D_MODEL))],
        out_specs=row,
        compiler_params=_params(1),
        name="ffn1",
    )(x, g, wg, wu, wd)


def _mix_ffn2(x, at, bt, wo, g, wg, wu, wd, gf):
    s = x.shape[0]
    row = pl.BlockSpec((FFN_ROWS, D_MODEL), lambda i: (i, 0))
    headt = pl.BlockSpec((FFN_ROWS // TILE, HEADS * MLA_V, TILE), lambda i: (i, 0, 0))
    return pl.pallas_call(
        _mix_ffn2_kernel,
        out_shape=jax.ShapeDtypeStruct((s, D_MODEL), F32),
        grid=(s // FFN_ROWS,),
        in_specs=[row, headt, headt, _const_spec((D_MODEL, D_MODEL)), _const_spec((1, D_MODEL)),
                  _const_spec((D_MODEL, D_FF)), _const_spec((D_MODEL, D_FF)),
                  _const_spec((D_FF, D_MODEL)), _const_spec((1, D_MODEL))],
        out_specs=row,
        compiler_params=_params(1),
        name="mix_ffn2",
    )(x, at, bt, wo, g, wg, wu, wd, gf)


def _proj(x, pos_row, inv, gm, win, gq, wuq, gkv, wk, wv):
    s = x.shape[0]
    nb = s // TILE
    tok_t = lambda rows: pl.BlockSpec((1, rows, TILE), lambda i: (i, 0, 0))
    tok = lambda cols: pl.BlockSpec((TILE, cols), lambda i: (i, 0))
    n_q = HEADS * (MLA_NOPE + MLA_ROPE)
    out_shape = (
        jax.ShapeDtypeStruct((nb, n_q, TILE), BF16),
        jax.ShapeDtypeStruct((s, HEADS * MLA_NOPE), BF16),
        jax.ShapeDtypeStruct((s, LANES), BF16),
        jax.ShapeDtypeStruct((nb, HEADS * MLA_V, TILE), BF16),
        jax.ShapeDtypeStruct((nb, HEADS * MOBA_DIM, TILE), BF16),
        jax.ShapeDtypeStruct((s, HEADS * MOBA_DIM), BF16),
        jax.ShapeDtypeStruct((nb, HEADS * MOBA_DIM, TILE), BF16),
        jax.ShapeDtypeStruct((nb, HEADS, nb, TILE), F32),
    )
    out_specs = (
        tok_t(n_q), tok(HEADS * MLA_NOPE), tok(LANES), tok_t(HEADS * MLA_V),
        tok_t(HEADS * MOBA_DIM), tok(HEADS * MOBA_DIM), tok_t(HEADS * MOBA_DIM),
        pl.BlockSpec((None, HEADS, nb, TILE), lambda i: (i, 0, 0, 0)),
    )
    in_specs = [
        tok(D_MODEL), pl.BlockSpec((1, TILE), lambda i: (0, i)), _const_spec((ROPE_HALF, 1)),
        _const_spec((1, D_MODEL)), _const_spec((D_MODEL, D_IN_PAD)),
        _const_spec((1, MLA_Q_RANK)), _const_spec((MLA_Q_RANK, n_q)),
        _const_spec((1, MLA_KV_RANK)), _const_spec((MLA_KV_RANK, HEADS * MLA_NOPE)),
        _const_spec((MLA_KV_RANK, HEADS * MLA_V)),
    ]
    return pl.pallas_call(
        _proj_kernel,
        out_shape=out_shape,
        grid=(nb,),
        in_specs=in_specs,
        out_specs=out_specs,
        scratch_shapes=[pltpu.VMEM((nb, HEADS * MOBA_DIM), F32)],
        compiler_params=_params(1),
        name="proj",
    )(x, pos_row, inv, gm, win, gq, wuq, gkv, wk, wv)


def _attn_scratch(q_rows):
    return [pltpu.VMEM((HEADS, q_rows, TILE), BF16), pltpu.VMEM((HEADS, 1, TILE), F32),
            pltpu.VMEM((HEADS, ACC_ROWS, TILE), F32), pltpu.VMEM((CARRY, TILE, TILE), F32)]


def _mla_attn(qt, kn, kr, vt):
    nb = qt.shape[0]
    n_q = HEADS * (MLA_NOPE + MLA_ROPE)
    return pl.pallas_call(
        _mla_kernel,
        out_shape=jax.ShapeDtypeStruct((nb, HEADS * MLA_V, TILE), BF16),
        grid=(nb,),
        in_specs=[pl.BlockSpec((1, n_q, TILE), lambda i: (i, 0, 0)),
                  _const_spec((nb, TILE, HEADS * MLA_NOPE), single=True),
                  _const_spec((nb, TILE, LANES), single=True),
                  _const_spec((nb, HEADS * MLA_V, TILE), single=True)],
        out_specs=pl.BlockSpec((1, HEADS * MLA_V, TILE), lambda i: (i, 0, 0)),
        scratch_shapes=_attn_scratch(2 * LANES),
        compiler_params=_params(1),
        name="mla_attn",
    )(qt, kn.reshape(nb, TILE, -1), kr.reshape(nb, TILE, LANES), vt)


def _moba_attn(c, fast, base, qt, k, vt, sel, pos_row, pos_rep, crow):
    nb = qt.shape[0]
    smem = pl.BlockSpec(memory_space=pltpu.SMEM)
    return pl.pallas_call(
        _moba_kernel,
        out_shape=jax.ShapeDtypeStruct((nb, HEADS * MOBA_DIM, TILE), BF16),
        grid=(nb,),
        in_specs=[smem, smem, smem,
                  pl.BlockSpec((1, HEADS * MOBA_DIM, TILE), lambda i: (i, 0, 0)),
                  _const_spec((nb, TILE, HEADS * MOBA_DIM), single=True),
                  _const_spec((nb, HEADS * MOBA_DIM, TILE), single=True),
                  pl.BlockSpec((None, HEADS, nb, TILE), lambda i: (i, 0, 0, 0)),
                  pl.BlockSpec((1, TILE), lambda i: (0, i)),
                  _const_spec((nb, TILE, LANES), single=True),
                  _const_spec((HEADS, AUX_ROWS, TILE), single=True)],
        out_specs=pl.BlockSpec((1, HEADS * MOBA_DIM, TILE), lambda i: (i, 0, 0)),
        scratch_shapes=_attn_scratch(LANES),
        compiler_params=_params(1),
        name="moba_attn",
    )(c, fast, base, qt, k.reshape(nb, TILE, -1), vt, sel, pos_row, pos_rep.reshape(nb, TILE, LANES), crow)


def _alibi_tables(positions, s):
    nb = s // TILE
    pos = positions.reshape(s)
    c = jnp.exp2(-8.0 * jnp.arange(1, HEADS + 1, dtype=F32) / HEADS) * LOG2E
    c1 = c.astype(BF16).astype(F32)
    c2 = (c - c1).astype(BF16).astype(F32)
    c3 = (c - c1 - c2).astype(BF16).astype(F32)
    rows = jnp.stack([POS_SPLIT * c1, POS_SPLIT * c2, POS_SPLIT * c3, c1, c2, c3], axis=1)
    crow = jnp.pad(rows, ((0, 0), (0, AUX_ROWS - rows.shape[1])))
    crow = jnp.broadcast_to(crow[:, :, None], (HEADS, AUX_ROWS, TILE))
    blk = pos.reshape(nb, TILE)
    base = blk.min(axis=1)
    past_max = jnp.concatenate([base[:1], lax.cummax(blk.max(axis=1))[:-1]])
    in_range = (pos.max() - pos.min()) < POS_LIMIT
    fast = jnp.logical_and(base >= past_max, in_range).astype(jnp.int32)
    return c, fast, base.astype(F32), crow


def kernel(x, positions, ffn1_norm, ffn1_w_gate, ffn1_w_up, ffn1_w_down, mix_norm, w_in, mla_q_norm,
           mla_w_uq, mla_kv_norm, mla_w_ukv, w_out, ffn2_norm, ffn2_w_gate, ffn2_w_up, ffn2_w_down,
           final_norm):
    b, s, d = x.shape
    assert b == 1 and d == D_MODEL and s % FFN_ROWS == 0 and w_in.shape[0] == 1
    xs = x[0]
    pos_row = positions.reshape(1, s)
    pos_f = positions.reshape(s, 1).astype(F32)
    pos_rep = jnp.broadcast_to(pos_f, (s, LANES))
    inv = jnp.power(ROPE_BASE, -jnp.arange(ROPE_HALF, dtype=F32) * 2.0 / MLA_ROPE).reshape(ROPE_HALF, 1)
    c_alibi, fast, base, crow = _alibi_tables(positions, s)

    wi = w_in[0]
    c_kr = MLA_Q_RANK + MLA_KV_RANK
    win = jnp.concatenate(
        [wi[:, :c_kr], wi[:, c_kr + MLA_ROPE:], wi[:, c_kr:c_kr + MLA_ROPE],
         jnp.zeros((D_MODEL, LANES - MLA_ROPE), F32)], axis=1).astype(BF16)
    wq3 = mla_w_uq[0].reshape(MLA_Q_RANK, HEADS, MLA_NOPE + MLA_ROPE)
    wuq = jnp.concatenate([wq3[:, :, :MLA_NOPE].reshape(MLA_Q_RANK, -1),
                           wq3[:, :, MLA_NOPE:].reshape(MLA_Q_RANK, -1)], axis=1).astype(BF16)
    wkv3 = mla_w_ukv[0].reshape(MLA_KV_RANK, HEADS, MLA_NOPE + MLA_V)
    wk = wkv3[:, :, :MLA_NOPE].reshape(MLA_KV_RANK, -1).astype(BF16)
    wv = wkv3[:, :, MLA_NOPE:].reshape(MLA_KV_RANK, -1).astype(BF16)
    row = lambda g: g.reshape(1, -1)

    x1 = _ffn1(xs, row(ffn1_norm[0]), ffn1_w_gate[0].astype(BF16), ffn1_w_up[0].astype(BF16),
               ffn1_w_down[0].astype(BF16))
    qa, ka, kr, va, qb, kb, vb, sel = _proj(x1, pos_row, inv, row(mix_norm[0]), win,
                                            row(mla_q_norm[0]), wuq, row(mla_kv_norm[0]), wk, wv)
    at = _mla_attn(qa, ka, kr, va)
    bt = _moba_attn(c_alibi, fast, base, qb, kb, vb, sel, pos_row.astype(F32), pos_rep, crow)
    out = _mix_ffn2(x1, at, bt, w_out[0].astype(BF16), row(ffn2_norm[0]), ffn2_w_gate[0].astype(BF16),
                    ffn2_w_up[0].astype(BF16), ffn2_w_down[0].astype(BF16), row(final_norm))
    return out[None]
```

```python
import math

import jax
import jax.numpy as jnp
from jax import lax
from jax.experimental import pallas as pl
from jax.experimental.pallas import tpu as pltpu

F32 = jnp.float32
BF16 = jnp.bfloat16

D_MODEL = 1024
D_FF = 2816
HEADS = 8
MLA_NOPE = 64
MLA_ROPE = 32
MLA_V = 64
MLA_Q_RANK = 384
MLA_KV_RANK = 256
ROPE_BASE = 10000.0
ROPE_HALF = MLA_ROPE // 2
MOBA_DIM = 64
MOBA_BLOCK = 256
MOBA_TOPK = 3
EPS = 1e-6
NEG_INF = -1e30
LOG2E = math.log2(math.e)

LANES = 128
TILE = MOBA_BLOCK
FFN_ROWS = 512
FFN_CHUNK = 256
VMEM_LIMIT = 56 * 1024 * 1024
LOOKAHEAD = 5
CARRY = 5
PAST_UNROLL = 8
HEAD_W = 64
ONES_ROWS = 16
ACC_ROWS = HEAD_W + ONES_ROWS
AUX_ROWS = 16
AUX_MASK_ROW = 6
POS_SPLIT = 128
POS_LIMIT = POS_SPLIT * 256
assert MLA_NOPE == HEAD_W and MOBA_DIM == HEAD_W and MLA_V == HEAD_W and LOOKAHEAD < HEADS

C_CQ = 0
C_CKV = C_CQ + MLA_Q_RANK
C_MQ = C_CKV + MLA_KV_RANK
C_MK = C_MQ + HEADS * MOBA_DIM
C_MV = C_MK + HEADS * MOBA_DIM
C_KR = C_MV + HEADS * MOBA_DIM
D_IN_PAD = C_KR + LANES


def _rms(x, g):
    y = x * lax.rsqrt(jnp.mean(x * x, axis=-1, keepdims=True) + EPS)
    return y * g


def _swiglu(h, wg_ref, wu_ref, wd_ref):
    acc = None
    for c in range(D_FF // FFN_CHUNK):
        sl = slice(c * FFN_CHUNK, (c + 1) * FFN_CHUNK)
        g = jnp.dot(h, wg_ref[:, sl], preferred_element_type=F32)
        u = jnp.dot(h, wu_ref[:, sl], preferred_element_type=F32)
        a = (g * jax.nn.sigmoid(g) * u).astype(BF16)
        d = jnp.dot(a, wd_ref[sl, :], preferred_element_type=F32)
        acc = d if acc is None else acc + d
    return acc


def _ffn1_kernel(x_ref, g_ref, wg_ref, wu_ref, wd_ref, o_ref):
    x = x_ref[...]
    h = _rms(x, g_ref[...]).astype(BF16)
    o_ref[...] = x + 0.5 * _swiglu(h, wg_ref, wu_ref, wd_ref)


def _mix_ffn2_kernel(x_ref, at_ref, bt_ref, wo_ref, g_ref, wg_ref, wu_ref, wd_ref, gf_ref, o_ref):
    half = HEADS * MLA_V
    tn = (((0,), (0,)), ((), ()))
    parts = []
    for j in range(FFN_ROWS // TILE):
        pa = lax.dot_general(at_ref[j], wo_ref[0:half, :], tn, preferred_element_type=F32)
        pb = lax.dot_general(bt_ref[j], wo_ref[half:2 * half, :], tn, preferred_element_type=F32)
        parts.append(pa + pb)
    x = x_ref[...] + jnp.concatenate(parts, axis=0)
    h = _rms(x, g_ref[...]).astype(BF16)
    y = x + 0.5 * _swiglu(h, wg_ref, wu_ref, wd_ref)
    o_ref[...] = _rms(y, gf_ref[...])


def _proj_kernel(x_ref, pos_ref, inv_ref, gm_ref, win_ref, gq_ref, wuq_ref, gkv_ref, wk_ref, wv_ref,
                 qa_ref, ka_ref, kr_ref, va_ref, qb_ref, kb_ref, vb_ref, sel_ref, km_ref):
    i = pl.program_id(0)
    nb = km_ref.shape[0]

    @pl.when(i == 0)
    def _():
        km_ref[...] = jnp.zeros_like(km_ref)

    h = _rms(x_ref[...], gm_ref[...]).astype(BF16)
    u = jnp.dot(h, win_ref[...], preferred_element_type=F32)

    ang = inv_ref[...] * pos_ref[...].astype(F32)
    cos, sin = jnp.cos(ang), jnp.sin(ang)

    def rope_t(xt):
        x1, x2 = xt[:ROPE_HALF], xt[ROPE_HALF:]
        return x1 * cos - x2 * sin, x1 * sin + x2 * cos

    cqn = _rms(u[:, C_CQ:C_CQ + MLA_Q_RANK], gq_ref[...]).astype(BF16)
    q = jnp.dot(cqn, wuq_ref[...], preferred_element_type=F32)
    qt = q.T * ((MLA_NOPE + MLA_ROPE) ** -0.5 * LOG2E)
    n_nope = HEADS * MLA_NOPE
    qa_ref[0, 0:n_nope, :] = qt[:n_nope].astype(BF16)
    for hd in range(HEADS):
        r0 = n_nope + hd * MLA_ROPE
        r1, r2 = rope_t(qt[r0:r0 + MLA_ROPE])
        qa_ref[0, r0:r0 + ROPE_HALF, :] = r1.astype(BF16)
        qa_ref[0, r0 + ROPE_HALF:r0 + MLA_ROPE, :] = r2.astype(BF16)

    ckvn = _rms(u[:, C_CKV:C_CKV + MLA_KV_RANK], gkv_ref[...]).astype(BF16)
    ka_ref[...] = jnp.dot(ckvn, wk_ref[...], preferred_element_type=F32).astype(BF16)
    v = jnp.dot(ckvn, wv_ref[...], preferred_element_type=F32)
    krt = u[:, C_KR:C_KR + LANES].T
    k1, k2 = rope_t(krt[:MLA_ROPE])
    kr_ref[...] = jnp.concatenate(
        [k1, k2, jnp.zeros((LANES - MLA_ROPE, TILE), F32)], axis=0).T.astype(BF16)
    va_ref[0] = v.T.astype(BF16)

    mq = u[:, C_MQ:C_MQ + HEADS * MOBA_DIM]
    mk = u[:, C_MK:C_MK + HEADS * MOBA_DIM]
    mv = u[:, C_MV:C_MV + HEADS * MOBA_DIM]
    mqt = mq.T
    qb_ref[0] = (mqt * (MOBA_DIM ** -0.5 * LOG2E)).astype(BF16)
    kb_ref[...] = mk.astype(BF16)
    vb_ref[0] = mv.T.astype(BF16)

    row = lax.broadcasted_iota(jnp.int32, (nb, TILE), 0)
    for hd in range(HEADS):
        gate = jnp.dot(km_ref[:, hd * MOBA_DIM:(hd + 1) * MOBA_DIM],
                       mqt[hd * MOBA_DIM:(hd + 1) * MOBA_DIM],
                       preferred_element_type=F32, precision=lax.Precision.HIGHEST)
        g = jnp.where(row < i, gate, NEG_INF)
        chosen = row == i
        for r in range(MOBA_TOPK):
            mx = jnp.max(g, axis=0, keepdims=True)
            idx = jnp.min(jnp.where(g == mx, row, nb), axis=0, keepdims=True)
            hit = row == idx
            chosen = jnp.logical_or(chosen, jnp.logical_and(hit, i > r))
            g = jnp.where(hit, -jnp.inf, g)
        sel_ref[hd] = jnp.where(chosen, 0.0, NEG_INF).astype(F32)
    km_ref[pl.ds(i, 1), :] = jnp.mean(mk, axis=0, keepdims=True)


def _online_update(hd, s, vt, m_ref, acc_ref):
    m_old = m_ref[hd]
    m_new = jnp.maximum(m_old, jnp.max(s, axis=0, keepdims=True))
    p = jnp.exp2(s - m_new).astype(BF16)
    alpha = jnp.exp2(m_old - m_new)
    vt1 = jnp.concatenate([vt, jnp.ones((ONES_ROWS, TILE), BF16)], axis=0)
    acc_ref[hd] = alpha * acc_ref[hd] + jnp.dot(vt1, p, preferred_element_type=F32)
    m_ref[hd] = m_new


def _init_state(m_ref, acc_ref):
    m_ref[...] = jnp.full(m_ref.shape, NEG_INF, F32)
    acc_ref[...] = jnp.zeros(acc_ref.shape, F32)


def _write_out(o_ref, acc_ref):
    for hd in range(HEADS):
        acc = acc_ref[hd]
        o_ref[0, hd * HEAD_W:(hd + 1) * HEAD_W, :] = (acc[:HEAD_W] / acc[HEAD_W:HEAD_W + 1]).astype(o_ref.dtype)


def _pair_pad(x, hd):
    z = jnp.zeros_like(x)
    return jnp.concatenate([x, z] if hd % 2 == 0 else [z, x], axis=0)


def _pipelined_tile(scores_now, scores_next, finish, s_ref):
    fresh = {hd: scores_now(hd) for hd in range(CARRY, LOOKAHEAD)}
    for t in range(HEADS):
        s_t = s_ref[t] if t < CARRY else fresh.pop(t)
        ahead = t + LOOKAHEAD
        if ahead < HEADS:
            fresh[ahead] = scores_now(ahead)
        elif ahead - HEADS < CARRY and scores_next is not None:
            s_ref[ahead - HEADS] = scores_next(ahead - HEADS)
        finish(t, s_t)


def _for_past_tiles(count, body):
    def group(i, c):
        for u in range(PAST_UNROLL):
            body(PAST_UNROLL * i + u)
        return c

    lax.fori_loop(0, count // PAST_UNROLL, group, 0)
    size = PAST_UNROLL // 2
    while size >= 1:
        start = (count // (2 * size)) * (2 * size)

        @pl.when((count // size) % 2 == 1)
        def _(start=start, size=size):
            for u in range(size):
                body(start + u)

        size //= 2


def _causal_mask():
    key = lax.broadcasted_iota(jnp.int32, (TILE, TILE), 0)
    qry = lax.broadcasted_iota(jnp.int32, (TILE, TILE), 1)
    return key <= qry


def _mla_kernel(qt_ref, kn_ref, kr_ref, vt_ref, o_ref, qp_ref, m_ref, acc_ref, s_ref):
    qi = pl.program_id(0)
    n_nope = HEADS * MLA_NOPE
    zrope = jnp.zeros((LANES - MLA_ROPE, TILE), BF16)
    for hd in range(HEADS):
        nope = qt_ref[0, hd * MLA_NOPE:(hd + 1) * MLA_NOPE, :]
        rope = qt_ref[0, n_nope + hd * MLA_ROPE:n_nope + (hd + 1) * MLA_ROPE, :]
        qp_ref[hd] = jnp.concatenate([_pair_pad(nope, hd), rope, zrope], axis=0)
    _init_state(m_ref, acc_ref)

    def scores(n):
        def f(hd):
            j = hd // 2
            kcat = jnp.concatenate([kn_ref[n, :, j * LANES:(j + 1) * LANES], kr_ref[n]], axis=1)
            return jnp.dot(kcat, qp_ref[hd], preferred_element_type=F32)
        return f

    def finish(n, mask):
        def f(hd, s):
            if mask is not None:
                s = jnp.where(mask, s, NEG_INF)
            _online_update(hd, s, vt_ref[n, hd * HEAD_W:(hd + 1) * HEAD_W, :], m_ref, acc_ref)
        return f

    for hd in range(CARRY):
        s_ref[hd] = scores(0)(hd)

    def past(n):
        _pipelined_tile(scores(n), scores(n + 1), finish(n, None), s_ref)

    _for_past_tiles(qi, past)
    _pipelined_tile(scores(qi), None, finish(qi, _causal_mask()), s_ref)
    _write_out(o_ref, acc_ref)


def _moba_kernel(c_ref, fast_ref, base_ref, qt_ref, k_ref, vt_ref, sel_ref, pq_ref, pk_ref, crow_ref, o_ref,
                 qp_ref, m_ref, acc_ref, s_ref):
    qi = pl.program_id(0)
    for hd in range(HEADS):
        qp_ref[hd] = _pair_pad(qt_ref[0, hd * MOBA_DIM:(hd + 1) * MOBA_DIM, :], hd)
    _init_state(m_ref, acc_ref)
    base = base_ref[qi]
    pq = pq_ref[...] - base
    row16 = lax.broadcasted_iota(jnp.int32, (AUX_ROWS, TILE), 0)
    lane = lax.broadcasted_iota(jnp.int32, (TILE, LANES), 1)
    zrows = jnp.zeros((LANES - AUX_ROWS, TILE), BF16)

    def kpair(n, hd):
        j = hd // 2
        return k_ref[n, :, j * LANES:(j + 1) * LANES]

    def scores_plain(n):
        def f(hd):
            return jnp.dot(kpair(n, hd), qp_ref[hd], preferred_element_type=F32)
        return f

    def scores_linear(n):
        rk = pk_ref[n] - base
        hi = jnp.floor(rk * (1.0 / POS_SPLIT))
        lo = rk - POS_SPLIT * hi
        aux = jnp.where(lane < 3, hi, jnp.where(lane < 6, lo, jnp.where(lane == 6, 1.0, 0.0))).astype(BF16)

        def f(hd):
            blk = jnp.where(row16 == AUX_MASK_ROW, sel_ref[hd, pl.ds(n, 1), :], crow_ref[hd])
            rhs = jnp.concatenate([qp_ref[hd], blk.astype(BF16), zrows], axis=0)
            lhs = jnp.concatenate([kpair(n, hd), aux], axis=1)
            return jnp.dot(lhs, rhs, preferred_element_type=F32)
        return f

    def finish_general(n, own):
        pk = pk_ref[n] - base
        dist = jnp.abs(jnp.concatenate([pk] * (TILE // LANES), axis=1) - pq)
        mask = _causal_mask() if own else None

        def f(hd, s):
            s = s - c_ref[hd] * dist + c_ref[hd] * pq
            if own:
                s = jnp.where(mask, s, NEG_INF)
            else:
                s = s + sel_ref[hd, pl.ds(n, 1), :]
            _online_update(hd, s, vt_ref[n, hd * HEAD_W:(hd + 1) * HEAD_W, :], m_ref, acc_ref)
        return f

    def finish_linear(n):
        def f(hd, s):
            _online_update(hd, s, vt_ref[n, hd * HEAD_W:(hd + 1) * HEAD_W, :], m_ref, acc_ref)
        return f

    for hd in range(CARRY):
        s_ref[hd] = scores_plain(qi)(hd)
    _pipelined_tile(scores_plain(qi), None, finish_general(qi, True), s_ref)

    @pl.when(fast_ref[qi] == 1)
    def _():
        for hd in range(CARRY):
            s_ref[hd] = scores_linear(0)(hd)
        _for_past_tiles(qi, lambda n: _pipelined_tile(scores_linear(n), scores_linear(n + 1),
                                                      finish_linear(n), s_ref))

    @pl.when(fast_ref[qi] != 1)
    def _():
        for hd in range(CARRY):
            s_ref[hd] = scores_plain(0)(hd)

        def past(n, c):
            _pipelined_tile(scores_plain(n), scores_plain(n + 1), finish_general(n, False), s_ref)
            return c

        lax.fori_loop(0, qi, past, 0)

    _write_out(o_ref, acc_ref)


def _const_spec(shape, single=False):
    zeros = (0,) * len(shape)
    if single:
        return pl.BlockSpec(shape, lambda *_: zeros, pipeline_mode=pl.Buffered(1))
    return pl.BlockSpec(shape, lambda *_: zeros)


def _params(n_axes):
    return pltpu.CompilerParams(dimension_semantics=("arbitrary",) * n_axes,
                                vmem_limit_bytes=VMEM_LIMIT)


def _ffn1(x, g, wg, wu, wd):
    s = x.shape[0]
    row = pl.BlockSpec((FFN_ROWS, D_MODEL), lambda i: (i, 0))
    return pl.pallas_call(
        _ffn1_kernel,
        out_shape=jax.ShapeDtypeStruct((s, D_MODEL), F32),
        grid=(s // FFN_ROWS,),
        in_specs=[row, _const_spec((1, D_MODEL)), _const_spec((D_MODEL, D_FF)),
                  _const_spec((D_MODEL, D_FF)), _const_spec((D_FF, D_MODEL))],
        out_specs=row,
        compiler_params=_params(1),
        name="ffn1",
    )(x, g, wg, wu, wd)


def _mix_ffn2(x, at, bt, wo, g, wg, wu, wd, gf):
    s = x.shape[0]
    row = pl.BlockSpec((FFN_ROWS, D_MODEL), lambda i: (i, 0))
    headt = pl.BlockSpec((FFN_ROWS // TILE, HEADS * MLA_V, TILE), lambda i: (i, 0, 0))
    return pl.pallas_call(
        _mix_ffn2_kernel,
        out_shape=jax.ShapeDtypeStruct((s, D_MODEL), F32),
        grid=(s // FFN_ROWS,),
        in_specs=[row, headt, headt, _const_spec((D_MODEL, D_MODEL)), _const_spec((1, D_MODEL)),
                  _const_spec((D_MODEL, D_FF)), _const_spec((D_MODEL, D_FF)),
                  _const_spec((D_FF, D_MODEL)), _const_spec((1, D_MODEL))],
        out_specs=row,
        compiler_params=_params(1),
        name="mix_ffn2",
    )(x, at, bt, wo, g, wg, wu, wd, gf)


def _proj(x, pos_row, inv, gm, win, gq, wuq, gkv, wk, wv):
    s = x.shape[0]
    nb = s // TILE
    tok_t = lambda rows: pl.BlockSpec((1, rows, TILE), lambda i: (i, 0, 0))
    tok = lambda cols: pl.BlockSpec((TILE, cols), lambda i: (i, 0))
    n_q = HEADS * (MLA_NOPE + MLA_ROPE)
    out_shape = (
        jax.ShapeDtypeStruct((nb, n_q, TILE), BF16),
        jax.ShapeDtypeStruct((s, HEADS * MLA_NOPE), BF16),
        jax.ShapeDtypeStruct((s, LANES), BF16),
        jax.ShapeDtypeStruct((nb, HEADS * MLA_V, TILE), BF16),
        jax.ShapeDtypeStruct((nb, HEADS * MOBA_DIM, TILE), BF16),
        jax.ShapeDtypeStruct((s, HEADS * MOBA_DIM), BF16),
        jax.ShapeDtypeStruct((nb, HEADS * MOBA_DIM, TILE), BF16),
        jax.ShapeDtypeStruct((nb, HEADS, nb, TILE), F32),
    )
    out_specs = (
        tok_t(n_q), tok(HEADS * MLA_NOPE), tok(LANES), tok_t(HEADS * MLA_V),
        tok_t(HEADS * MOBA_DIM), tok(HEADS * MOBA_DIM), tok_t(HEADS * MOBA_DIM),
        pl.BlockSpec((None, HEADS, nb, TILE), lambda i: (i, 0, 0, 0)),
    )
    in_specs = [
        tok(D_MODEL), pl.BlockSpec((1, TILE), lambda i: (0, i)), _const_spec((ROPE_HALF, 1)),
        _const_spec((1, D_MODEL)), _const_spec((D_MODEL, D_IN_PAD)),
        _const_spec((1, MLA_Q_RANK)), _const_spec((MLA_Q_RANK, n_q)),
        _const_spec((1, MLA_KV_RANK)), _const_spec((MLA_KV_RANK, HEADS * MLA_NOPE)),
        _const_spec((MLA_KV_RANK, HEADS * MLA_V)),
    ]
    return pl.pallas_call(
        _proj_kernel,
        out_shape=out_shape,
        grid=(nb,),
        in_specs=in_specs,
        out_specs=out_specs,
        scratch_shapes=[pltpu.VMEM((nb, HEADS * MOBA_DIM), F32)],
        compiler_params=_params(1),
        name="proj",
    )(x, pos_row, inv, gm, win, gq, wuq, gkv, wk, wv)


def _attn_scratch(q_rows):
    return [pltpu.VMEM((HEADS, q_rows, TILE), BF16), pltpu.VMEM((HEADS, 1, TILE), F32),
            pltpu.VMEM((HEADS, ACC_ROWS, TILE), F32), pltpu.VMEM((CARRY, TILE, TILE), F32)]


def _mla_attn(qt, kn, kr, vt):
    nb = qt.shape[0]
    n_q = HEADS * (MLA_NOPE + MLA_ROPE)
    return pl.pallas_call(
        _mla_kernel,
        out_shape=jax.ShapeDtypeStruct((nb, HEADS * MLA_V, TILE), BF16),
        grid=(nb,),
        in_specs=[pl.BlockSpec((1, n_q, TILE), lambda i: (i, 0, 0)),
                  _const_spec((nb, TILE, HEADS * MLA_NOPE), single=True),
                  _const_spec((nb, TILE, LANES), single=True),
                  _const_spec((nb, HEADS * MLA_V, TILE), single=True)],
        out_specs=pl.BlockSpec((1, HEADS * MLA_V, TILE), lambda i: (i, 0, 0)),
        scratch_shapes=_attn_scratch(2 * LANES),
        compiler_params=_params(1),
        name="mla_attn",
    )(qt, kn.reshape(nb, TILE, -1), kr.reshape(nb, TILE, LANES), vt)


def _moba_attn(c, fast, base, qt, k, vt, sel, pos_row, pos_rep, crow):
    nb = qt.shape[0]
    smem = pl.BlockSpec(memory_space=pltpu.SMEM)
    return pl.pallas_call(
        _moba_kernel,
        out_shape=jax.ShapeDtypeStruct((nb, HEADS * MOBA_DIM, TILE), BF16),
        grid=(nb,),
        in_specs=[smem, smem, smem,
                  pl.BlockSpec((1, HEADS * MOBA_DIM, TILE), lambda i: (i, 0, 0)),
                  _const_spec((nb, TILE, HEADS * MOBA_DIM), single=True),
                  _const_spec((nb, HEADS * MOBA_DIM, TILE), single=True),
                  pl.BlockSpec((None, HEADS, nb, TILE), lambda i: (i, 0, 0, 0)),
                  pl.BlockSpec((1, TILE), lambda i: (0, i)),
                  _const_spec((nb, TILE, LANES), single=True),
                  _const_spec((HEADS, AUX_ROWS, TILE), single=True)],
        out_specs=pl.BlockSpec((1, HEADS * MOBA_DIM, TILE), lambda i: (i, 0, 0)),
        scratch_shapes=_attn_scratch(LANES),
        compiler_params=_params(1),
        name="moba_attn",
    )(c, fast, base, qt, k.reshape(nb, TILE, -1), vt, sel, pos_row, pos_rep.reshape(nb, TILE, LANES), crow)


def _alibi_tables(positions, s):
    nb = s // TILE
    pos = positions.reshape(s)
    c = jnp.exp2(-8.0 * jnp.arange(1, HEADS + 1, dtype=F32) / HEADS) * LOG2E
    c1 = c.astype(BF16).astype(F32)
    c2 = (c - c1).astype(BF16).astype(F32)
    c3 = (c - c1 - c2).astype(BF16).astype(F32)
    rows = jnp.stack([POS_SPLIT * c1, POS_SPLIT * c2, POS_SPLIT * c3, c1, c2, c3], axis=1)
    crow = jnp.pad(rows, ((0, 0), (0, AUX_ROWS - rows.shape[1])))
    crow = jnp.broadcast_to(crow[:, :, None], (HEADS, AUX_ROWS, TILE))
    blk = pos.reshape(nb, TILE)
    base = blk.min(axis=1)
    past_max = jnp.concatenate([base[:1], lax.cummax(blk.max(axis=1))[:-1]])
    in_range = (pos.max() - pos.min()) < POS_LIMIT
    fast = jnp.logical_and(base >= past_max, in_range).astype(jnp.int32)
    return c, fast, base.astype(F32), crow


def kernel(x, positions, ffn1_norm, ffn1_w_gate, ffn1_w_up, ffn1_w_down, mix_norm, w_in, mla_q_norm,
           mla_w_uq, mla_kv_norm, mla_w_ukv, w_out, ffn2_norm, ffn2_w_gate, ffn2_w_up, ffn2_w_down,
           final_norm):
    b, s, d = x.shape
    assert b == 1 and d == D_MODEL and s % FFN_ROWS == 0 and w_in.shape[0] == 1
    xs = x[0]
    pos_row = positions.reshape(1, s)
    pos_f = positions.reshape(s, 1).astype(F32)
    pos_rep = jnp.broadcast_to(pos_f, (s, LANES))
    inv = jnp.power(ROPE_BASE, -jnp.arange(ROPE_HALF, dtype=F32) * 2.0 / MLA_ROPE).reshape(ROPE_HALF, 1)
    c_alibi, fast, base, crow = _alibi_tables(positions, s)

    wi = w_in[0]
    c_kr = MLA_Q_RANK + MLA_KV_RANK
    win = jnp.concatenate(
        [wi[:, :c_kr], wi[:, c_kr + MLA_ROPE:], wi[:, c_kr:c_kr + MLA_ROPE],
         jnp.zeros((D_MODEL, LANES - MLA_ROPE), F32)], axis=1).astype(BF16)
    wq3 = mla_w_uq[0].reshape(MLA_Q_RANK, HEADS, MLA_NOPE + MLA_ROPE)
    wuq = jnp.concatenate([wq3[:, :, :MLA_NOPE].reshape(MLA_Q_RANK, -1),
                           wq3[:, :, MLA_NOPE:].reshape(MLA_Q_RANK, -1)], axis=1).astype(BF16)
    wkv3 = mla_w_ukv[0].reshape(MLA_KV_RANK, HEADS, MLA_NOPE + MLA_V)
    wk = wkv3[:, :, :MLA_NOPE].reshape(MLA_KV_RANK, -1).astype(BF16)
    wv = wkv3[:, :, MLA_NOPE:].reshape(MLA_KV_RANK, -1).astype(BF16)
    row = lambda g: g.reshape(1, -1)

    x1 = _ffn1(xs, row(ffn1_norm[0]), ffn1_w_gate[0].astype(BF16), ffn1_w_up[0].astype(BF16),
               ffn1_w_down[0].astype(BF16))
    qa, ka, kr, va, qb, kb, vb, sel = _proj(x1, pos_row, inv, row(mix_norm[0]), win,
                                            row(mla_q_norm[0]), wuq, row(mla_kv_norm[0]), wk, wv)
    at = _mla_attn(qa, ka, kr, va)
    bt = _moba_attn(c_alibi, fast, base, qb, kb, vb, sel, pos_row.astype(F32), pos_rep, crow)
    out = _mix_ffn2(x1, at, bt, w_out[0].astype(BF16), row(ffn2_norm[0]), ffn2_w_gate[0].astype(BF16),
                    ffn2_w_up[0].astype(BF16), ffn2_w_down[0].astype(BF16), row(final_norm))
    return out[None]
```

```python
import math

import jax
import jax.numpy as jnp
from jax import lax
from jax.experimental import pallas as pl
from jax.experimental.pallas import tpu as pltpu

F32 = jnp.float32
BF16 = jnp.bfloat16

D_MODEL = 1024
D_FF = 2816
HEADS = 8
MLA_NOPE = 64
MLA_ROPE = 32
MLA_V = 64
MLA_Q_RANK = 384
MLA_KV_RANK = 256
ROPE_BASE = 10000.0
ROPE_HALF = MLA_ROPE // 2
MOBA_DIM = 64
MOBA_BLOCK = 256
MOBA_TOPK = 3
EPS = 1e-6
NEG_INF = -1e30
LOG2E = math.log2(math.e)

LANES = 128
TILE = MOBA_BLOCK
FFN_ROWS = 512
FFN_CHUNK = 256
VMEM_LIMIT = 56 * 1024 * 1024
LOOKAHEAD = 5
CARRY = 5
PAST_UNROLL = 8
HEAD_W = 64
ONES_ROWS = 16
ACC_ROWS = HEAD_W + ONES_ROWS
AUX_ROWS = 16
AUX_MASK_ROW = 6
POS_SPLIT = 128
POS_LIMIT = POS_SPLIT * 256
assert MLA_NOPE == HEAD_W and MOBA_DIM == HEAD_W and MLA_V == HEAD_W and LOOKAHEAD < HEADS

C_CQ = 0
C_CKV = C_CQ + MLA_Q_RANK
C_MQ = C_CKV + MLA_KV_RANK
C_MK = C_MQ + HEADS * MOBA_DIM
C_MV = C_MK + HEADS * MOBA_DIM
C_KR = C_MV + HEADS * MOBA_DIM
D_IN_PAD = C_KR + LANES


def _rms(x, g):
    y = x * lax.rsqrt(jnp.mean(x * x, axis=-1, keepdims=True) + EPS)
    return y * g


def _swiglu(h, wg_ref, wu_ref, wd_ref):
    acc = None
    for c in range(D_FF // FFN_CHUNK):
        sl = slice(c * FFN_CHUNK, (c + 1) * FFN_CHUNK)
        g = jnp.dot(h, wg_ref[:, sl], preferred_element_type=F32)
        u = jnp.dot(h, wu_ref[:, sl], preferred_element_type=F32)
        a = (g * jax.nn.sigmoid(g) * u).astype(BF16)
        d = jnp.dot(a, wd_ref[sl, :], preferred_element_type=F32)
        acc = d if acc is None else acc + d
    return acc


def _ffn1_kernel(x_ref, g_ref, wg_ref, wu_ref, wd_ref, o_ref):
    x = x_ref[...]
    h = _rms(x, g_ref[...]).astype(BF16)
    o_ref[...] = x + 0.5 * _swiglu(h, wg_ref, wu_ref, wd_ref)


def _mix_ffn2_kernel(x_ref, at_ref, bt_ref, wo_ref, g_ref, wg_ref, wu_ref, wd_ref, gf_ref, o_ref):
    half = HEADS * MLA_V
    tn = (((0,), (0,)), ((), ()))
    parts = []
    for j in range(FFN_ROWS // TILE):
        pa = lax.dot_general(at_ref[j], wo_ref[0:half, :], tn, preferred_element_type=F32)
        pb = lax.dot_general(bt_ref[j], wo_ref[half:2 * half, :], tn, preferred_element_type=F32)
        parts.append(pa + pb)
    x = x_ref[...] + jnp.concatenate(parts, axis=0)
    h = _rms(x, g_ref[...]).astype(BF16)
    y = x + 0.5 * _swiglu(h, wg_ref, wu_ref, wd_ref)
    o_ref[...] = _rms(y, gf_ref[...])


def _proj_kernel(x_ref, pos_ref, inv_ref, gm_ref, win_ref, gq_ref, wuq_ref, gkv_ref, wk_ref, wv_ref,
                 qa_ref, ka_ref, kr_ref, va_ref, qb_ref, kb_ref, vb_ref, sel_ref, km_ref):
    i = pl.program_id(0)
    nb = km_ref.shape[0]

    @pl.when(i == 0)
    def _():
        km_ref[...] = jnp.zeros_like(km_ref)

    h = _rms(x_ref[...], gm_ref[...]).astype(BF16)
    u = jnp.dot(h, win_ref[...], preferred_element_type=F32)

    ang = inv_ref[...] * pos_ref[...].astype(F32)
    cos, sin = jnp.cos(ang), jnp.sin(ang)

    def rope_t(xt):
        x1, x2 = xt[:ROPE_HALF], xt[ROPE_HALF:]
        return x1 * cos - x2 * sin, x1 * sin + x2 * cos

    cqn = _rms(u[:, C_CQ:C_CQ + MLA_Q_RANK], gq_ref[...]).astype(BF16)
    q = jnp.dot(cqn, wuq_ref[...], preferred_element_type=F32)
    qt = q.T * ((MLA_NOPE + MLA_ROPE) ** -0.5 * LOG2E)
    n_nope = HEADS * MLA_NOPE
    qa_ref[0, 0:n_nope, :] = qt[:n_nope].astype(BF16)
    for hd in range(HEADS):
        r0 = n_nope + hd * MLA_ROPE
        r1, r2 = rope_t(qt[r0:r0 + MLA_ROPE])
        qa_ref[0, r0:r0 + ROPE_HALF, :] = r1.astype(BF16)
        qa_ref[0, r0 + ROPE_HALF:r0 + MLA_ROPE, :] = r2.astype(BF16)

    ckvn = _rms(u[:, C_CKV:C_CKV + MLA_KV_RANK], gkv_ref[...]).astype(BF16)
    ka_ref[...] = jnp.dot(ckvn, wk_ref[...], preferred_element_type=F32).astype(BF16)
    v = jnp.dot(ckvn, wv_ref[...], preferred_element_type=F32)
    krt = u[:, C_KR:C_KR + LANES].T
    k1, k2 = rope_t(krt[:MLA_ROPE])
    kr_ref[...] = jnp.concatenate(
        [k1, k2, jnp.zeros((LANES - MLA_ROPE, TILE), F32)], axis=0).T.astype(BF16)
    va_ref[0] = v.T.astype(BF16)

    mq = u[:, C_MQ:C_MQ + HEADS * MOBA_DIM]
    mk = u[:, C_MK:C_MK + HEADS * MOBA_DIM]
    mv = u[:, C_MV:C_MV + HEADS * MOBA_DIM]
    mqt = mq.T
    qb_ref[0] = (mqt * (MOBA_DIM ** -0.5 * LOG2E)).astype(BF16)
    kb_ref[...] = mk.astype(BF16)
    vb_ref[0] = mv.T.astype(BF16)

    row = lax.broadcasted_iota(jnp.int32, (nb, TILE), 0)
    for hd in range(HEADS):
        gate = jnp.dot(km_ref[:, hd * MOBA_DIM:(hd + 1) * MOBA_DIM],
                       mqt[hd * MOBA_DIM:(hd + 1) * MOBA_DIM],
                       preferred_element_type=F32, precision=lax.Precision.HIGHEST)
        g = jnp.where(row < i, gate, NEG_INF)
        chosen = row == i
        for r in range(MOBA_TOPK):
            mx = jnp.max(g, axis=0, keepdims=True)
            idx = jnp.min(jnp.where(g == mx, row, nb), axis=0, keepdims=True)
            hit = row == idx
            chosen = jnp.logical_or(chosen, jnp.logical_and(hit, i > r))
            g = jnp.where(hit, -jnp.inf, g)
        sel_ref[hd] = jnp.where(chosen, 0.0, NEG_INF).astype(F32)
    km_ref[pl.ds(i, 1), :] = jnp.mean(mk, axis=0, keepdims=True)


def _online_update(hd, s, vt, m_ref, acc_ref):
    m_old = m_ref[hd]
    m_new = jnp.maximum(m_old, jnp.max(s, axis=0, keepdims=True))
    p = jnp.exp2(s - m_new).astype(BF16)
    alpha = jnp.exp2(m_old - m_new)
    vt1 = jnp.concatenate([vt, jnp.ones((ONES_ROWS, TILE), BF16)], axis=0)
    acc_ref[hd] = alpha * acc_ref[hd] + jnp.dot(vt1, p, preferred_element_type=F32)
    m_ref[hd] = m_new


def _init_state(m_ref, acc_ref):
    m_ref[...] = jnp.full(m_ref.shape, NEG_INF, F32)
    acc_ref[...] = jnp.zeros(acc_ref.shape, F32)


def _write_out(o_ref, acc_ref):
    for hd in range(HEADS):
        acc = acc_ref[hd]
        o_ref[0, hd * HEAD_W:(hd + 1) * HEAD_W, :] = (acc[:HEAD_W] / acc[HEAD_W:HEAD_W + 1]).astype(o_ref.dtype)


def _pair_pad(x, hd):
    z = jnp.zeros_like(x)
    return jnp.concatenate([x, z] if hd % 2 == 0 else [z, x], axis=0)


def _pipelined_tile(scores_now, scores_next, finish, s_ref):
    fresh = {hd: scores_now(hd) for hd in range(CARRY, LOOKAHEAD)}
    for t in range(HEADS):
        s_t = s_ref[t] if t < CARRY else fresh.pop(t)
        ahead = t + LOOKAHEAD
        if ahead < HEADS:
            fresh[ahead] = scores_now(ahead)
        elif ahead - HEADS < CARRY and scores_next is not None:
            s_ref[ahead - HEADS] = scores_next(ahead - HEADS)
        finish(t, s_t)


def _for_past_tiles(count, body):
    def group(i, c):
        for u in range(PAST_UNROLL):
            body(PAST_UNROLL * i + u)
        return c

    lax.fori_loop(0, count // PAST_UNROLL, group, 0)
    size = PAST_UNROLL // 2
    while size >= 1:
        start = (count // (2 * size)) * (2 * size)

        @pl.when((count // size) % 2 == 1)
        def _(start=start, size=size):
            for u in range(size):
                body(start + u)

        size //= 2


def _causal_mask():
    key = lax.broadcasted_iota(jnp.int32, (TILE, TILE), 0)
    qry = lax.broadcasted_iota(jnp.int32, (TILE, TILE), 1)
    return key <= qry


def _mla_kernel(qt_ref, kn_ref, kr_ref, vt_ref, o_ref, qp_ref, m_ref, acc_ref, s_ref):
    qi = pl.program_id(0)
    n_nope = HEADS * MLA_NOPE
    zrope = jnp.zeros((LANES - MLA_ROPE, TILE), BF16)
    for hd in range(HEADS):
        nope = qt_ref[0, hd * MLA_NOPE:(hd + 1) * MLA_NOPE, :]
        rope = qt_ref[0, n_nope + hd * MLA_ROPE:n_nope + (hd + 1) * MLA_ROPE, :]
        qp_ref[hd] = jnp.concatenate([_pair_pad(nope, hd), rope, zrope], axis=0)
    _init_state(m_ref, acc_ref)

    def scores(n):
        def f(hd):
            j = hd // 2
            kcat = jnp.concatenate([kn_ref[n, :, j * LANES:(j + 1) * LANES], kr_ref[n]], axis=1)
            return jnp.dot(kcat, qp_ref[hd], preferred_element_type=F32)
        return f

    def finish(n, mask):
        def f(hd, s):
            if mask is not None:
                s = jnp.where(mask, s, NEG_INF)
            _online_update(hd, s, vt_ref[n, hd * HEAD_W:(hd + 1) * HEAD_W, :], m_ref, acc_ref)
        return f

    for hd in range(CARRY):
        s_ref[hd] = scores(0)(hd)

    def past(n):
        _pipelined_tile(scores(n), scores(n + 1), finish(n, None), s_ref)

    _for_past_tiles(qi, past)
    _pipelined_tile(scores(qi), None, finish(qi, _causal_mask()), s_ref)
    _write_out(o_ref, acc_ref)


def _moba_kernel(c_ref, fast_ref, base_ref, qt_ref, k_ref, vt_ref, sel_ref, pq_ref, pk_ref, crow_ref, o_ref,
                 qp_ref, m_ref, acc_ref, s_ref, aux_ref):
    qi = pl.program_id(0)
    for hd in range(HEADS):
        qp_ref[hd] = _pair_pad(qt_ref[0, hd * MOBA_DIM:(hd + 1) * MOBA_DIM, :], hd)
    _init_state(m_ref, acc_ref)
    base = base_ref[qi]
    pq = pq_ref[...] - base
    row16 = lax.broadcasted_iota(jnp.int32, (AUX_ROWS, TILE), 0)
    lane = lax.broadcasted_iota(jnp.int32, (TILE, LANES), 1)
    zrows = jnp.zeros((LANES - AUX_ROWS, TILE), BF16)

    def kpair(n, hd):
        j = hd // 2
        return k_ref[n, :, j * LANES:(j + 1) * LANES]

    def scores_plain(n):
        def f(hd):
            return jnp.dot(kpair(n, hd), qp_ref[hd], preferred_element_type=F32)
        return f

    def make_aux(n):
        rk = pk_ref[n] - base
        hi = jnp.floor(rk * (1.0 / POS_SPLIT))
        lo = rk - POS_SPLIT * hi
        return jnp.where(lane < 3, hi, jnp.where(lane < 6, lo, jnp.where(lane == 6, 1.0, 0.0))).astype(BF16)

    def scores_linear(n, aux):
        def f(hd):
            blk = jnp.where(row16 == AUX_MASK_ROW, sel_ref[hd, pl.ds(n, 1), :], crow_ref[hd])
            rhs = jnp.concatenate([qp_ref[hd], blk.astype(BF16), zrows], axis=0)
            lhs = jnp.concatenate([kpair(n, hd), aux], axis=1)
            return jnp.dot(lhs, rhs, preferred_element_type=F32)
        return f

    def finish_general(n, own):
        pk = pk_ref[n] - base
        dist = jnp.abs(jnp.concatenate([pk] * (TILE // LANES), axis=1) - pq)
        mask = _causal_mask() if own else None

        def f(hd, s):
            s = s - c_ref[hd] * dist + c_ref[hd] * pq
            if own:
                s = jnp.where(mask, s, NEG_INF)
            else:
                s = s + sel_ref[hd, pl.ds(n, 1), :]
            _online_update(hd, s, vt_ref[n, hd * HEAD_W:(hd + 1) * HEAD_W, :], m_ref, acc_ref)
        return f

    def finish_linear(n):
        def f(hd, s):
            _online_update(hd, s, vt_ref[n, hd * HEAD_W:(hd + 1) * HEAD_W, :], m_ref, acc_ref)
        return f

    for hd in range(CARRY):
        s_ref[hd] = scores_plain(qi)(hd)
    _pipelined_tile(scores_plain(qi), None, finish_general(qi, True), s_ref)

    @pl.when(fast_ref[qi] == 1)
    def _():
        aux0 = make_aux(0)
        aux_ref[...] = aux0
        for hd in range(CARRY):
            s_ref[hd] = scores_linear(0, aux0)(hd)

        def past(n):
            aux_now = aux_ref[...]
            aux_next = make_aux(n + 1)
            aux_ref[...] = aux_next
            _pipelined_tile(scores_linear(n, aux_now), scores_linear(n + 1, aux_next), finish_linear(n), s_ref)

        _for_past_tiles(qi, past)

    @pl.when(fast_ref[qi] != 1)
    def _():
        for hd in range(CARRY):
            s_ref[hd] = scores_plain(0)(hd)

        def past(n, c):
            _pipelined_tile(scores_plain(n), scores_plain(n + 1), finish_general(n, False), s_ref)
            return c

        lax.fori_loop(0, qi, past, 0)

    _write_out(o_ref, acc_ref)


def _const_spec(shape, single=False):
    zeros = (0,) * len(shape)
    if single:
        return pl.BlockSpec(shape, lambda *_: zeros, pipeline_mode=pl.Buffered(1))
    return pl.BlockSpec(shape, lambda *_: zeros)


def _params(n_axes):
    return pltpu.CompilerParams(dimension_semantics=("arbitrary",) * n_axes,
                                vmem_limit_bytes=VMEM_LIMIT)


def _ffn1(x, g, wg, wu, wd):
    s = x.shape[0]
    row = pl.BlockSpec((FFN_ROWS, D_MODEL), lambda i: (i, 0))
    return pl.pallas_call(
        _ffn1_kernel,
        out_shape=jax.ShapeDtypeStruct((s, D_MODEL), F32),
        grid=(s // FFN_ROWS,),
        in_specs=[row, _const_spec((1, D_MODEL)), _const_spec((D_MODEL, D_FF)),
                  _const_spec((D_MODEL, D_FF)), _const_spec((D_FF, D_MODEL))],
        out_specs=row,
        compiler_params=_params(1),
        name="ffn1",
    )(x, g, wg, wu, wd)


def _mix_ffn2(x, at, bt, wo, g, wg, wu, wd, gf):
    s = x.shape[0]
    row = pl.BlockSpec((FFN_ROWS, D_MODEL), lambda i: (i, 0))
    headt = pl.BlockSpec((FFN_ROWS // TILE, HEADS * MLA_V, TILE), lambda i: (i, 0, 0))
    return pl.pallas_call(
        _mix_ffn2_kernel,
        out_shape=jax.ShapeDtypeStruct((s, D_MODEL), F32),
        grid=(s // FFN_ROWS,),
        in_specs=[row, headt, headt, _const_spec((D_MODEL, D_MODEL)), _const_spec((1, D_MODEL)),
                  _const_spec((D_MODEL, D_FF)), _const_spec((D_MODEL, D_FF)),
                  _const_spec((D_FF, D_MODEL)), _const_spec((1, D_MODEL))],
        out_specs=row,
        compiler_params=_params(1),
        name="mix_ffn2",
    )(x, at, bt, wo, g, wg, wu, wd, gf)


def _proj(x, pos_row, inv, gm, win, gq, wuq, gkv, wk, wv):
    s = x.shape[0]
    nb = s // TILE
    tok_t = lambda rows: pl.BlockSpec((1, rows, TILE), lambda i: (i, 0, 0))
    tok = lambda cols: pl.BlockSpec((TILE, cols), lambda i: (i, 0))
    n_q = HEADS * (MLA_NOPE + MLA_ROPE)
    out_shape = (
        jax.ShapeDtypeStruct((nb, n_q, TILE), BF16),
        jax.ShapeDtypeStruct((s, HEADS * MLA_NOPE), BF16),
        jax.ShapeDtypeStruct((s, LANES), BF16),
        jax.ShapeDtypeStruct((nb, HEADS * MLA_V, TILE), BF16),
        jax.ShapeDtypeStruct((nb, HEADS * MOBA_DIM, TILE), BF16),
        jax.ShapeDtypeStruct((s, HEADS * MOBA_DIM), BF16),
        jax.ShapeDtypeStruct((nb, HEADS * MOBA_DIM, TILE), BF16),
        jax.ShapeDtypeStruct((nb, HEADS, nb, TILE), F32),
    )
    out_specs = (
        tok_t(n_q), tok(HEADS * MLA_NOPE), tok(LANES), tok_t(HEADS * MLA_V),
        tok_t(HEADS * MOBA_DIM), tok(HEADS * MOBA_DIM), tok_t(HEADS * MOBA_DIM),
        pl.BlockSpec((None, HEADS, nb, TILE), lambda i: (i, 0, 0, 0)),
    )
    in_specs = [
        tok(D_MODEL), pl.BlockSpec((1, TILE), lambda i: (0, i)), _const_spec((ROPE_HALF, 1)),
        _const_spec((1, D_MODEL)), _const_spec((D_MODEL, D_IN_PAD)),
        _const_spec((1, MLA_Q_RANK)), _const_spec((MLA_Q_RANK, n_q)),
        _const_spec((1, MLA_KV_RANK)), _const_spec((MLA_KV_RANK, HEADS * MLA_NOPE)),
        _const_spec((MLA_KV_RANK, HEADS * MLA_V)),
    ]
    return pl.pallas_call(
        _proj_kernel,
        out_shape=out_shape,
        grid=(nb,),
        in_specs=in_specs,
        out_specs=out_specs,
        scratch_shapes=[pltpu.VMEM((nb, HEADS * MOBA_DIM), F32)],
        compiler_params=_params(1),
        name="proj",
    )(x, pos_row, inv, gm, win, gq, wuq, gkv, wk, wv)


def _attn_scratch(q_rows):
    return [pltpu.VMEM((HEADS, q_rows, TILE), BF16), pltpu.VMEM((HEADS, 1, TILE), F32),
            pltpu.VMEM((HEADS, ACC_ROWS, TILE), F32), pltpu.VMEM((CARRY, TILE, TILE), F32)]


def _mla_attn(qt, kn, kr, vt):
    nb = qt.shape[0]
    n_q = HEADS * (MLA_NOPE + MLA_ROPE)
    return pl.pallas_call(
        _mla_kernel,
        out_shape=jax.ShapeDtypeStruct((nb, HEADS * MLA_V, TILE), BF16),
        grid=(nb,),
        in_specs=[pl.BlockSpec((1, n_q, TILE), lambda i: (i, 0, 0)),
                  _const_spec((nb, TILE, HEADS * MLA_NOPE), single=True),
                  _const_spec((nb, TILE, LANES), single=True),
                  _const_spec((nb, HEADS * MLA_V, TILE), single=True)],
        out_specs=pl.BlockSpec((1, HEADS * MLA_V, TILE), lambda i: (i, 0, 0)),
        scratch_shapes=_attn_scratch(2 * LANES),
        compiler_params=_params(1),
        name="mla_attn",
    )(qt, kn.reshape(nb, TILE, -1), kr.reshape(nb, TILE, LANES), vt)


def _moba_attn(c, fast, base, qt, k, vt, sel, pos_row, pos_rep, crow):
    nb = qt.shape[0]
    smem = pl.BlockSpec(memory_space=pltpu.SMEM)
    return pl.pallas_call(
        _moba_kernel,
        out_shape=jax.ShapeDtypeStruct((nb, HEADS * MOBA_DIM, TILE), BF16),
        grid=(nb,),
        in_specs=[smem, smem, smem,
                  pl.BlockSpec((1, HEADS * MOBA_DIM, TILE), lambda i: (i, 0, 0)),
                  _const_spec((nb, TILE, HEADS * MOBA_DIM), single=True),
                  _const_spec((nb, HEADS * MOBA_DIM, TILE), single=True),
                  pl.BlockSpec((None, HEADS, nb, TILE), lambda i: (i, 0, 0, 0)),
                  pl.BlockSpec((1, TILE), lambda i: (0, i)),
                  _const_spec((nb, TILE, LANES), single=True),
                  _const_spec((HEADS, AUX_ROWS, TILE), single=True)],
        out_specs=pl.BlockSpec((1, HEADS * MOBA_DIM, TILE), lambda i: (i, 0, 0)),
        scratch_shapes=_attn_scratch(LANES) + [pltpu.VMEM((TILE, LANES), BF16)],
        compiler_params=_params(1),
        name="moba_attn",
    )(c, fast, base, qt, k.reshape(nb, TILE, -1), vt, sel, pos_row, pos_rep.reshape(nb, TILE, LANES), crow)


def _alibi_tables(positions, s):
    nb = s // TILE
    pos = positions.reshape(s)
    c = jnp.exp2(-8.0 * jnp.arange(1, HEADS + 1, dtype=F32) / HEADS) * LOG2E
    c1 = c.astype(BF16).astype(F32)
    c2 = (c - c1).astype(BF16).astype(F32)
    c3 = (c - c1 - c2).astype(BF16).astype(F32)
    rows = jnp.stack([POS_SPLIT * c1, POS_SPLIT * c2, POS_SPLIT * c3, c1, c2, c3], axis=1)
    crow = jnp.pad(rows, ((0, 0), (0, AUX_ROWS - rows.shape[1])))
    crow = jnp.broadcast_to(crow[:, :, None], (HEADS, AUX_ROWS, TILE))
    blk = pos.reshape(nb, TILE)
    base = blk.min(axis=1)
    past_max = jnp.concatenate([base[:1], lax.cummax(blk.max(axis=1))[:-1]])
    in_range = (pos.max() - pos.min()) < POS_LIMIT
    fast = jnp.logical_and(base >= past_max, in_range).astype(jnp.int32)
    return c, fast, base.astype(F32), crow


def kernel(x, positions, ffn1_norm, ffn1_w_gate, ffn1_w_up, ffn1_w_down, mix_norm, w_in, mla_q_norm,
           mla_w_uq, mla_kv_norm, mla_w_ukv, w_out, ffn2_norm, ffn2_w_gate, ffn2_w_up, ffn2_w_down,
           final_norm):
    b, s, d = x.shape
    assert b == 1 and d == D_MODEL and s % FFN_ROWS == 0 and w_in.shape[0] == 1
    xs = x[0]
    pos_row = positions.reshape(1, s)
    pos_f = positions.reshape(s, 1).astype(F32)
    pos_rep = jnp.broadcast_to(pos_f, (s, LANES))
    inv = jnp.power(ROPE_BASE, -jnp.arange(ROPE_HALF, dtype=F32) * 2.0 / MLA_ROPE).reshape(ROPE_HALF, 1)
    c_alibi, fast, base, crow = _alibi_tables(positions, s)

    wi = w_in[0]
    c_kr = MLA_Q_RANK + MLA_KV_RANK
    win = jnp.concatenate(
        [wi[:, :c_kr], wi[:, c_kr + MLA_ROPE:], wi[:, c_kr:c_kr + MLA_ROPE],
         jnp.zeros((D_MODEL, LANES - MLA_ROPE), F32)], axis=1).astype(BF16)
    wq3 = mla_w_uq[0].reshape(MLA_Q_RANK, HEADS, MLA_NOPE + MLA_ROPE)
    wuq = jnp.concatenate([wq3[:, :, :MLA_NOPE].reshape(MLA_Q_RANK, -1),
                           wq3[:, :, MLA_NOPE:].reshape(MLA_Q_RANK, -1)], axis=1).astype(BF16)
    wkv3 = mla_w_ukv[0].reshape(MLA_KV_RANK, HEADS, MLA_NOPE + MLA_V)
    wk = wkv3[:, :, :MLA_NOPE].reshape(MLA_KV_RANK, -1).astype(BF16)
    wv = wkv3[:, :, MLA_NOPE:].reshape(MLA_KV_RANK, -1).astype(BF16)
    row = lambda g: g.reshape(1, -1)

    x1 = _ffn1(xs, row(ffn1_norm[0]), ffn1_w_gate[0].astype(BF16), ffn1_w_up[0].astype(BF16),
               ffn1_w_down[0].astype(BF16))
    qa, ka, kr, va, qb, kb, vb, sel = _proj(x1, pos_row, inv, row(mix_norm[0]), win,
                                            row(mla_q_norm[0]), wuq, row(mla_kv_norm[0]), wk, wv)
    at = _mla_attn(qa, ka, kr, va)
    bt = _moba_attn(c_alibi, fast, base, qb, kb, vb, sel, pos_row.astype(F32), pos_rep, crow)
    out = _mix_ffn2(x1, at, bt, w_out[0].astype(BF16), row(ffn2_norm[0]), ffn2_w_gate[0].astype(BF16),
                    ffn2_w_up[0].astype(BF16), ffn2_w_down[0].astype(BF16), row(final_norm))
    return out[None]
```

```python
import math

import jax
import jax.numpy as jnp
from jax import lax
from jax.experimental import pallas as pl
from jax.experimental.pallas import tpu as pltpu

F32 = jnp.float32
BF16 = jnp.bfloat16

D_MODEL = 1024
D_FF = 2816
HEADS = 8
MLA_NOPE = 64
MLA_ROPE = 32
MLA_V = 64
MLA_Q_RANK = 384
MLA_KV_RANK = 256
ROPE_BASE = 10000.0
ROPE_HALF = MLA_ROPE // 2
MOBA_DIM = 64
MOBA_BLOCK = 256
MOBA_TOPK = 3
EPS = 1e-6
NEG_INF = -1e30
LOG2E = math.log2(math.e)

LANES = 128
TILE = MOBA_BLOCK
FFN_ROWS = 512
FFN_CHUNK = 256
VMEM_LIMIT = 56 * 1024 * 1024
LOOKAHEAD = 5
CARRY = 5
PAST_UNROLL = 16
HEAD_W = 64
ONES_ROWS = 16
ACC_ROWS = HEAD_W + ONES_ROWS
AUX_ROWS = 16
AUX_MASK_ROW = 6
POS_SPLIT = 128
POS_LIMIT = POS_SPLIT * 256
assert MLA_NOPE == HEAD_W and MOBA_DIM == HEAD_W and MLA_V == HEAD_W and LOOKAHEAD < HEADS

C_CQ = 0
C_CKV = C_CQ + MLA_Q_RANK
C_MQ = C_CKV + MLA_KV_RANK
C_MK = C_MQ + HEADS * MOBA_DIM
C_MV = C_MK + HEADS * MOBA_DIM
C_KR = C_MV + HEADS * MOBA_DIM
D_IN_PAD = C_KR + LANES


def _rms(x, g):
    y = x * lax.rsqrt(jnp.mean(x * x, axis=-1, keepdims=True) + EPS)
    return y * g


def _swiglu(h, wg_ref, wu_ref, wd_ref):
    acc = None
    for c in range(D_FF // FFN_CHUNK):
        sl = slice(c * FFN_CHUNK, (c + 1) * FFN_CHUNK)
        g = jnp.dot(h, wg_ref[:, sl], preferred_element_type=F32)
        u = jnp.dot(h, wu_ref[:, sl], preferred_element_type=F32)
        a = (g * jax.nn.sigmoid(g) * u).astype(BF16)
        d = jnp.dot(a, wd_ref[sl, :], preferred_element_type=F32)
        acc = d if acc is None else acc + d
    return acc


def _ffn1_kernel(x_ref, g_ref, wg_ref, wu_ref, wd_ref, o_ref):
    x = x_ref[...]
    h = _rms(x, g_ref[...]).astype(BF16)
    o_ref[...] = x + 0.5 * _swiglu(h, wg_ref, wu_ref, wd_ref)


def _mix_ffn2_kernel(x_ref, at_ref, bt_ref, wo_ref, g_ref, wg_ref, wu_ref, wd_ref, gf_ref, o_ref):
    half = HEADS * MLA_V
    tn = (((0,), (0,)), ((), ()))
    parts = []
    for j in range(FFN_ROWS // TILE):
        pa = lax.dot_general(at_ref[j], wo_ref[0:half, :], tn, preferred_element_type=F32)
        pb = lax.dot_general(bt_ref[j], wo_ref[half:2 * half, :], tn, preferred_element_type=F32)
        parts.append(pa + pb)
    x = x_ref[...] + jnp.concatenate(parts, axis=0)
    h = _rms(x, g_ref[...]).astype(BF16)
    y = x + 0.5 * _swiglu(h, wg_ref, wu_ref, wd_ref)
    o_ref[...] = _rms(y, gf_ref[...])


def _proj_kernel(x_ref, pos_ref, inv_ref, gm_ref, win_ref, gq_ref, wuq_ref, gkv_ref, wk_ref, wv_ref,
                 qa_ref, ka_ref, kr_ref, va_ref, qb_ref, kb_ref, vb_ref, sel_ref, km_ref):
    i = pl.program_id(0)
    nb = km_ref.shape[0]

    @pl.when(i == 0)
    def _():
        km_ref[...] = jnp.zeros_like(km_ref)

    h = _rms(x_ref[...], gm_ref[...]).astype(BF16)
    u = jnp.dot(h, win_ref[...], preferred_element_type=F32)

    ang = inv_ref[...] * pos_ref[...].astype(F32)
    cos, sin = jnp.cos(ang), jnp.sin(ang)

    def rope_t(xt):
        x1, x2 = xt[:ROPE_HALF], xt[ROPE_HALF:]
        return x1 * cos - x2 * sin, x1 * sin + x2 * cos

    cqn = _rms(u[:, C_CQ:C_CQ + MLA_Q_RANK], gq_ref[...]).astype(BF16)
    q = jnp.dot(cqn, wuq_ref[...], preferred_element_type=F32)
    qt = q.T * ((MLA_NOPE + MLA_ROPE) ** -0.5 * LOG2E)
    n_nope = HEADS * MLA_NOPE
    qa_ref[0, 0:n_nope, :] = qt[:n_nope].astype(BF16)
    for hd in range(HEADS):
        r0 = n_nope + hd * MLA_ROPE
        r1, r2 = rope_t(qt[r0:r0 + MLA_ROPE])
        qa_ref[0, r0:r0 + ROPE_HALF, :] = r1.astype(BF16)
        qa_ref[0, r0 + ROPE_HALF:r0 + MLA_ROPE, :] = r2.astype(BF16)

    ckvn = _rms(u[:, C_CKV:C_CKV + MLA_KV_RANK], gkv_ref[...]).astype(BF16)
    ka_ref[...] = jnp.dot(ckvn, wk_ref[...], preferred_element_type=F32).astype(BF16)
    v = jnp.dot(ckvn, wv_ref[...], preferred_element_type=F32)
    krt = u[:, C_KR:C_KR + LANES].T
    k1, k2 = rope_t(krt[:MLA_ROPE])
    kr_ref[...] = jnp.concatenate(
        [k1, k2, jnp.zeros((LANES - MLA_ROPE, TILE), F32)], axis=0).T.astype(BF16)
    va_ref[0] = v.T.astype(BF16)

    mq = u[:, C_MQ:C_MQ + HEADS * MOBA_DIM]
    mk = u[:, C_MK:C_MK + HEADS * MOBA_DIM]
    mv = u[:, C_MV:C_MV + HEADS * MOBA_DIM]
    mqt = mq.T
    qb_ref[0] = (mqt * (MOBA_DIM ** -0.5 * LOG2E)).astype(BF16)
    kb_ref[...] = mk.astype(BF16)
    vb_ref[0] = mv.T.astype(BF16)

    row = lax.broadcasted_iota(jnp.int32, (nb, TILE), 0)
    km = km_ref[...]
    km_hi = km.astype(BF16).astype(F32)
    km_lo = km - km_hi
    mq_hi = mqt.astype(BF16)
    mq_lo = (mqt - mq_hi.astype(F32)).astype(BF16)
    for hd in range(HEADS):
        sl = slice(hd * MOBA_DIM, (hd + 1) * MOBA_DIM)
        lhs = jnp.concatenate([km_hi[:, sl], km_lo[:, sl], km_hi[:, sl]], axis=1).astype(BF16)
        rhs = jnp.concatenate([mq_hi[sl], mq_hi[sl], mq_lo[sl]], axis=0)
        gate = jnp.dot(lhs, rhs, preferred_element_type=F32)
        g = jnp.where(row < i, gate, NEG_INF)
        chosen = row == i
        for r in range(MOBA_TOPK):
            mx = jnp.max(g, axis=0, keepdims=True)
            idx = jnp.min(jnp.where(g == mx, row, nb), axis=0, keepdims=True)
            hit = row == idx
            chosen = jnp.logical_or(chosen, jnp.logical_and(hit, i > r))
            g = jnp.where(hit, -jnp.inf, g)
        sel_ref[hd] = jnp.where(chosen, 0.0, NEG_INF).astype(F32)
    km_ref[pl.ds(i, 1), :] = jnp.mean(mk, axis=0, keepdims=True)


def _online_update(hd, s, vt, m_ref, acc_ref):
    m_old = m_ref[hd]
    m_new = jnp.maximum(m_old, jnp.max(s, axis=0, keepdims=True))
    p = jnp.exp2(s - m_new).astype(BF16)
    alpha = jnp.exp2(m_old - m_new)
    vt1 = jnp.concatenate([vt, jnp.ones((ONES_ROWS, TILE), BF16)], axis=0)
    acc_ref[hd] = alpha * acc_ref[hd] + jnp.dot(vt1, p, preferred_element_type=F32)
    m_ref[hd] = m_new


def _init_state(m_ref, acc_ref):
    m_ref[...] = jnp.full(m_ref.shape, NEG_INF, F32)
    acc_ref[...] = jnp.zeros(acc_ref.shape, F32)


def _write_out(o_ref, acc_ref):
    for hd in range(HEADS):
        acc = acc_ref[hd]
        o_ref[0, hd * HEAD_W:(hd + 1) * HEAD_W, :] = (acc[:HEAD_W] / acc[HEAD_W:HEAD_W + 1]).astype(o_ref.dtype)


def _pair_pad(x, hd):
    z = jnp.zeros_like(x)
    return jnp.concatenate([x, z] if hd % 2 == 0 else [z, x], axis=0)


def _pipelined_tile(scores_now, scores_next, finish, s_ref):
    fresh = {hd: scores_now(hd) for hd in range(CARRY, LOOKAHEAD)}
    for t in range(HEADS):
        s_t = s_ref[t] if t < CARRY else fresh.pop(t)
        ahead = t + LOOKAHEAD
        if ahead < HEADS:
            fresh[ahead] = scores_now(ahead)
        elif ahead - HEADS < CARRY and scores_next is not None:
            s_ref[ahead - HEADS] = scores_next(ahead - HEADS)
        finish(t, s_t)


def _for_past_tiles(count, body):
    def group(i, c):
        for u in range(PAST_UNROLL):
            body(PAST_UNROLL * i + u)
        return c

    lax.fori_loop(0, count // PAST_UNROLL, group, 0)
    size = PAST_UNROLL // 2
    while size >= 1:
        start = (count // (2 * size)) * (2 * size)

        @pl.when((count // size) % 2 == 1)
        def _(start=start, size=size):
            for u in range(size):
                body(start + u)

        size //= 2


def _causal_mask():
    key = lax.broadcasted_iota(jnp.int32, (TILE, TILE), 0)
    qry = lax.broadcasted_iota(jnp.int32, (TILE, TILE), 1)
    return key <= qry


def _mla_kernel(qt_ref, kn_ref, kr_ref, vt_ref, o_ref, qp_ref, m_ref, acc_ref, s_ref):
    qi = pl.program_id(0)
    n_nope = HEADS * MLA_NOPE
    zrope = jnp.zeros((LANES - MLA_ROPE, TILE), BF16)
    for hd in range(HEADS):
        nope = qt_ref[0, hd * MLA_NOPE:(hd + 1) * MLA_NOPE, :]
        rope = qt_ref[0, n_nope + hd * MLA_ROPE:n_nope + (hd + 1) * MLA_ROPE, :]
        qp_ref[hd] = jnp.concatenate([_pair_pad(nope, hd), rope, zrope], axis=0)
    _init_state(m_ref, acc_ref)

    def scores(n):
        def f(hd):
            j = hd // 2
            kcat = jnp.concatenate([kn_ref[n, :, j * LANES:(j + 1) * LANES], kr_ref[n]], axis=1)
            return jnp.dot(kcat, qp_ref[hd], preferred_element_type=F32)
        return f

    def finish(n, mask):
        def f(hd, s):
            if mask is not None:
                s = jnp.where(mask, s, NEG_INF)
            _online_update(hd, s, vt_ref[n, hd * HEAD_W:(hd + 1) * HEAD_W, :], m_ref, acc_ref)
        return f

    for hd in range(CARRY):
        s_ref[hd] = scores(0)(hd)

    def past(n):
        _pipelined_tile(scores(n), scores(n + 1), finish(n, None), s_ref)

    _for_past_tiles(qi, past)
    _pipelined_tile(scores(qi), None, finish(qi, _causal_mask()), s_ref)
    _write_out(o_ref, acc_ref)


def _moba_kernel(c_ref, fast_ref, base_ref, qt_ref, k_ref, vt_ref, sel_ref, pq_ref, pk_ref, crow_ref, o_ref,
                 qp_ref, m_ref, acc_ref, s_ref, aux_ref):
    qi = pl.program_id(0)
    for hd in range(HEADS):
        qp_ref[hd] = _pair_pad(qt_ref[0, hd * MOBA_DIM:(hd + 1) * MOBA_DIM, :], hd)
    _init_state(m_ref, acc_ref)
    base = base_ref[qi]
    pq = pq_ref[...] - base
    row16 = lax.broadcasted_iota(jnp.int32, (AUX_ROWS, TILE), 0)
    lane = lax.broadcasted_iota(jnp.int32, (TILE, LANES), 1)
    zrows = jnp.zeros((LANES - AUX_ROWS, TILE), BF16)

    def kpair(n, hd):
        j = hd // 2
        return k_ref[n, :, j * LANES:(j + 1) * LANES]

    def scores_plain(n):
        def f(hd):
            return jnp.dot(kpair(n, hd), qp_ref[hd], preferred_element_type=F32)
        return f

    def make_aux(n):
        rk = pk_ref[n] - base
        hi = jnp.floor(rk * (1.0 / POS_SPLIT))
        lo = rk - POS_SPLIT * hi
        return jnp.where(lane < 3, hi, jnp.where(lane < 6, lo, jnp.where(lane == 6, 1.0, 0.0))).astype(BF16)

    def scores_linear(n, aux):
        def f(hd):
            blk = jnp.where(row16 == AUX_MASK_ROW, sel_ref[hd, pl.ds(n, 1), :], crow_ref[hd])
            rhs = jnp.concatenate([qp_ref[hd], blk.astype(BF16), zrows], axis=0)
            lhs = jnp.concatenate([kpair(n, hd), aux], axis=1)
            return jnp.dot(lhs, rhs, preferred_element_type=F32)
        return f

    def finish_general(n, own):
        pk = pk_ref[n] - base
        dist = jnp.abs(jnp.concatenate([pk] * (TILE // LANES), axis=1) - pq)
        mask = _causal_mask() if own else None

        def f(hd, s):
            s = s - c_ref[hd] * dist + c_ref[hd] * pq
            if own:
                s = jnp.where(mask, s, NEG_INF)
            else:
                s = s + sel_ref[hd, pl.ds(n, 1), :]
            _online_update(hd, s, vt_ref[n, hd * HEAD_W:(hd + 1) * HEAD_W, :], m_ref, acc_ref)
        return f

    def finish_linear(n):
        def f(hd, s):
            _online_update(hd, s, vt_ref[n, hd * HEAD_W:(hd + 1) * HEAD_W, :], m_ref, acc_ref)
        return f

    for hd in range(CARRY):
        s_ref[hd] = scores_plain(qi)(hd)
    _pipelined_tile(scores_plain(qi), None, finish_general(qi, True), s_ref)

    @pl.when(fast_ref[qi] == 1)
    def _():
        aux0 = make_aux(0)
        aux_ref[...] = aux0
        for hd in range(CARRY):
            s_ref[hd] = scores_linear(0, aux0)(hd)

        def past(n):
            aux_now = aux_ref[...]
            aux_next = make_aux(n + 1)
            aux_ref[...] = aux_next
            _pipelined_tile(scores_linear(n, aux_now), scores_linear(n + 1, aux_next), finish_linear(n), s_ref)

        _for_past_tiles(qi, past)

    @pl.when(fast_ref[qi] != 1)
    def _():
        for hd in range(CARRY):
            s_ref[hd] = scores_plain(0)(hd)

        def past(n, c):
            _pipelined_tile(scores_plain(n), scores_plain(n + 1), finish_general(n, False), s_ref)
            return c

        lax.fori_loop(0, qi, past, 0)

    _write_out(o_ref, acc_ref)


def _const_spec(shape, single=False):
    zeros = (0,) * len(shape)
    if single:
        return pl.BlockSpec(shape, lambda *_: zeros, pipeline_mode=pl.Buffered(1))
    return pl.BlockSpec(shape, lambda *_: zeros)


def _params(n_axes):
    return pltpu.CompilerParams(dimension_semantics=("arbitrary",) * n_axes,
                                vmem_limit_bytes=VMEM_LIMIT)


def _ffn1(x, g, wg, wu, wd):
    s = x.shape[0]
    row = pl.BlockSpec((FFN_ROWS, D_MODEL), lambda i: (i, 0))
    return pl.pallas_call(
        _ffn1_kernel,
        out_shape=jax.ShapeDtypeStruct((s, D_MODEL), F32),
        grid=(s // FFN_ROWS,),
        in_specs=[row, _const_spec((1, D_MODEL)), _const_spec((D_MODEL, D_FF)),
                  _const_spec((D_MODEL, D_FF)), _const_spec((D_FF, D_MODEL))],
        out_specs=row,
        compiler_params=_params(1),
        name="ffn1",
    )(x, g, wg, wu, wd)


def _mix_ffn2(x, at, bt, wo, g, wg, wu, wd, gf):
    s = x.shape[0]
    row = pl.BlockSpec((FFN_ROWS, D_MODEL), lambda i: (i, 0))
    headt = pl.BlockSpec((FFN_ROWS // TILE, HEADS * MLA_V, TILE), lambda i: (i, 0, 0))
    return pl.pallas_call(
        _mix_ffn2_kernel,
        out_shape=jax.ShapeDtypeStruct((s, D_MODEL), F32),
        grid=(s // FFN_ROWS,),
        in_specs=[row, headt, headt, _const_spec((D_MODEL, D_MODEL)), _const_spec((1, D_MODEL)),
                  _const_spec((D_MODEL, D_FF)), _const_spec((D_MODEL, D_FF)),
                  _const_spec((D_FF, D_MODEL)), _const_spec((1, D_MODEL))],
        out_specs=row,
        compiler_params=_params(1),
        name="mix_ffn2",
    )(x, at, bt, wo, g, wg, wu, wd, gf)


def _proj(x, pos_row, inv, gm, win, gq, wuq, gkv, wk, wv):
    s = x.shape[0]
    nb = s // TILE
    tok_t = lambda rows: pl.BlockSpec((1, rows, TILE), lambda i: (i, 0, 0))
    tok = lambda cols: pl.BlockSpec((TILE, cols), lambda i: (i, 0))
    n_q = HEADS * (MLA_NOPE + MLA_ROPE)
    out_shape = (
        jax.ShapeDtypeStruct((nb, n_q, TILE), BF16),
        jax.ShapeDtypeStruct((s, HEADS * MLA_NOPE), BF16),
        jax.ShapeDtypeStruct((s, LANES), BF16),
        jax.ShapeDtypeStruct((nb, HEADS * MLA_V, TILE), BF16),
        jax.ShapeDtypeStruct((nb, HEADS * MOBA_DIM, TILE), BF16),
        jax.ShapeDtypeStruct((s, HEADS * MOBA_DIM), BF16),
        jax.ShapeDtypeStruct((nb, HEADS * MOBA_DIM, TILE), BF16),
        jax.ShapeDtypeStruct((nb, HEADS, nb, TILE), F32),
    )
    out_specs = (
        tok_t(n_q), tok(HEADS * MLA_NOPE), tok(LANES), tok_t(HEADS * MLA_V),
        tok_t(HEADS * MOBA_DIM), tok(HEADS * MOBA_DIM), tok_t(HEADS * MOBA_DIM),
        pl.BlockSpec((None, HEADS, nb, TILE), lambda i: (i, 0, 0, 0)),
    )
    in_specs = [
        tok(D_MODEL), pl.BlockSpec((1, TILE), lambda i: (0, i)), _const_spec((ROPE_HALF, 1)),
        _const_spec((1, D_MODEL)), _const_spec((D_MODEL, D_IN_PAD)),
        _const_spec((1, MLA_Q_RANK)), _const_spec((MLA_Q_RANK, n_q)),
        _const_spec((1, MLA_KV_RANK)), _const_spec((MLA_KV_RANK, HEADS * MLA_NOPE)),
        _const_spec((MLA_KV_RANK, HEADS * MLA_V)),
    ]
    return pl.pallas_call(
        _proj_kernel,
        out_shape=out_shape,
        grid=(nb,),
        in_specs=in_specs,
        out_specs=out_specs,
        scratch_shapes=[pltpu.VMEM((nb, HEADS * MOBA_DIM), F32)],
        compiler_params=_params(1),
        name="proj",
    )(x, pos_row, inv, gm, win, gq, wuq, gkv, wk, wv)


def _attn_scratch(q_rows):
    return [pltpu.VMEM((HEADS, q_rows, TILE), BF16), pltpu.VMEM((HEADS, 1, TILE), F32),
            pltpu.VMEM((HEADS, ACC_ROWS, TILE), F32), pltpu.VMEM((CARRY, TILE, TILE), F32)]


def _mla_attn(qt, kn, kr, vt):
    nb = qt.shape[0]
    n_q = HEADS * (MLA_NOPE + MLA_ROPE)
    return pl.pallas_call(
        _mla_kernel,
        out_shape=jax.ShapeDtypeStruct((nb, HEADS * MLA_V, TILE), BF16),
        grid=(nb,),
        in_specs=[pl.BlockSpec((1, n_q, TILE), lambda i: (i, 0, 0)),
                  _const_spec((nb, TILE, HEADS * MLA_NOPE), single=True),
                  _const_spec((nb, TILE, LANES), single=True),
                  _const_spec((nb, HEADS * MLA_V, TILE), single=True)],
        out_specs=pl.BlockSpec((1, HEADS * MLA_V, TILE), lambda i: (i, 0, 0)),
        scratch_shapes=_attn_scratch(2 * LANES),
        compiler_params=_params(1),
        name="mla_attn",
    )(qt, kn.reshape(nb, TILE, -1), kr.reshape(nb, TILE, LANES), vt)


def _moba_attn(c, fast, base, qt, k, vt, sel, pos_row, pos_rep, crow):
    nb = qt.shape[0]
    smem = pl.BlockSpec(memory_space=pltpu.SMEM)
    return pl.pallas_call(
        _moba_kernel,
        out_shape=jax.ShapeDtypeStruct((nb, HEADS * MOBA_DIM, TILE), BF16),
        grid=(nb,),
        in_specs=[smem, smem, smem,
                  pl.BlockSpec((1, HEADS * MOBA_DIM, TILE), lambda i: (i, 0, 0)),
                  _const_spec((nb, TILE, HEADS * MOBA_DIM), single=True),
                  _const_spec((nb, HEADS * MOBA_DIM, TILE), single=True),
                  pl.BlockSpec((None, HEADS, nb, TILE), lambda i: (i, 0, 0, 0)),
                  pl.BlockSpec((1, TILE), lambda i: (0, i)),
                  _const_spec((nb, TILE, LANES), single=True),
                  _const_spec((HEADS, AUX_ROWS, TILE), single=True)],
        out_specs=pl.BlockSpec((1, HEADS * MOBA_DIM, TILE), lambda i: (i, 0, 0)),
        scratch_shapes=_attn_scratch(LANES) + [pltpu.VMEM((TILE, LANES), BF16)],
        compiler_params=_params(1),
        name="moba_attn",
    )(c, fast, base, qt, k.reshape(nb, TILE, -1), vt, sel, pos_row, pos_rep.reshape(nb, TILE, LANES), crow)


def _alibi_tables(positions, s):
    nb = s // TILE
    pos = positions.reshape(s)
    c = jnp.exp2(-8.0 * jnp.arange(1, HEADS + 1, dtype=F32) / HEADS) * LOG2E
    c1 = c.astype(BF16).astype(F32)
    c2 = (c - c1).astype(BF16).astype(F32)
    c3 = (c - c1 - c2).astype(BF16).astype(F32)
    rows = jnp.stack([POS_SPLIT * c1, POS_SPLIT * c2, POS_SPLIT * c3, c1, c2, c3], axis=1)
    crow = jnp.pad(rows, ((0, 0), (0, AUX_ROWS - rows.shape[1])))
    crow = jnp.broadcast_to(crow[:, :, None], (HEADS, AUX_ROWS, TILE))
    blk = pos.reshape(nb, TILE)
    base = blk.min(axis=1)
    past_max = jnp.concatenate([base[:1], lax.cummax(blk.max(axis=1))[:-1]])
    in_range = (pos.max().astype(F32) - pos.min().astype(F32)) < POS_LIMIT
    fast = jnp.logical_and(base >= past_max, in_range).astype(jnp.int32)
    return c, fast, base.astype(F32), crow


def kernel(x, positions, ffn1_norm, ffn1_w_gate, ffn1_w_up, ffn1_w_down, mix_norm, w_in, mla_q_norm,
           mla_w_uq, mla_kv_norm, mla_w_ukv, w_out, ffn2_norm, ffn2_w_gate, ffn2_w_up, ffn2_w_down,
           final_norm):
    b, s, d = x.shape
    assert b == 1 and d == D_MODEL and s % FFN_ROWS == 0 and w_in.shape[0] == 1
    xs = x[0]
    pos_row = positions.reshape(1, s)
    pos_f = positions.reshape(s, 1).astype(F32)
    pos_rep = jnp.broadcast_to(pos_f, (s, LANES))
    inv = jnp.power(ROPE_BASE, -jnp.arange(ROPE_HALF, dtype=F32) * 2.0 / MLA_ROPE).reshape(ROPE_HALF, 1)
    c_alibi, fast, base, crow = _alibi_tables(positions, s)

    wi = w_in[0]
    c_kr = MLA_Q_RANK + MLA_KV_RANK
    win = jnp.concatenate(
        [wi[:, :c_kr], wi[:, c_kr + MLA_ROPE:], wi[:, c_kr:c_kr + MLA_ROPE],
         jnp.zeros((D_MODEL, LANES - MLA_ROPE), F32)], axis=1).astype(BF16)
    wq3 = mla_w_uq[0].reshape(MLA_Q_RANK, HEADS, MLA_NOPE + MLA_ROPE)
    wuq = jnp.concatenate([wq3[:, :, :MLA_NOPE].reshape(MLA_Q_RANK, -1),
                           wq3[:, :, MLA_NOPE:].reshape(MLA_Q_RANK, -1)], axis=1).astype(BF16)
    wkv3 = mla_w_ukv[0].reshape(MLA_KV_RANK, HEADS, MLA_NOPE + MLA_V)
    wk = wkv3[:, :, :MLA_NOPE].reshape(MLA_KV_RANK, -1).astype(BF16)
    wv = wkv3[:, :, MLA_NOPE:].reshape(MLA_KV_RANK, -1).astype(BF16)
    row = lambda g: g.reshape(1, -1)

    x1 = _ffn1(xs, row(ffn1_norm[0]), ffn1_w_gate[0].astype(BF16), ffn1_w_up[0].astype(BF16),
               ffn1_w_down[0].astype(BF16))
    qa, ka, kr, va, qb, kb, vb, sel = _proj(x1, pos_row, inv, row(mix_norm[0]), win,
                                            row(mla_q_norm[0]), wuq, row(mla_kv_norm[0]), wk, wv)
    at = _mla_attn(qa, ka, kr, va)
    bt = _moba_attn(c_alibi, fast, base, qb, kb, vb, sel, pos_row.astype(F32), pos_rep, crow)
    out = _mix_ffn2(x1, at, bt, w_out[0].astype(BF16), row(ffn2_norm[0]), ffn2_w_gate[0].astype(BF16),
                    ffn2_w_up[0].astype(BF16), ffn2_w_down[0].astype(BF16), row(final_norm))
    return out[None]
```
